```python
import math
import jax, jax.numpy as jnp
from jax import lax
import numpy as np

D_MODEL = 1024
BATCH = 8
SEQ = 4096
DEPTH = 2
DEC_BATCH = 128
DEC_SEQ = 1
PAST_LEN = 16384
PAGE_SIZE = 128

HEAD_DIM = 64
SSM_GROUP = 16
SSM_WIDTH = 512
SSM_GROUPS = SSM_WIDTH // SSM_GROUP
SSM_STATE = 64
SWA_WINDOW = 128
SWA_Q_HEADS = 8
SWA_KV_HEADS = 2
DIL_PAIRS = ((128, 1), (512, 4), (2048, 16))
DIL_HEADS = 4
BAND = 128
N_BRANCH = 3
D_FF = 4 * D_MODEL
PLE_DIM = 256
EPS = 1e-6

SWA_Q_W = SWA_Q_HEADS * HEAD_DIM
SWA_KV_W = SWA_KV_HEADS * HEAD_DIM
DIL_W = DIL_HEADS * HEAD_DIM
SPLITS = (SSM_WIDTH, SWA_Q_W, SWA_KV_W, SWA_KV_W) + (DIL_W,) * (3 * len(DIL_PAIRS)) + (D_MODEL,) * N_BRANCH
D_IN = sum(SPLITS)

kernel_name = 'hybrid_s5_swa_dilated_decoder_step'


def rmsnorm(x, g):
    xf = x.astype(jnp.float32)
    y = xf * lax.rsqrt(jnp.mean(xf * xf, axis=-1, keepdims=True) + EPS)
    return (y * g.astype(jnp.float32)).astype(x.dtype)


def cplx_affine_combine(e1, e2):
    a1r, a1i, b1r, b1i = e1
    a2r, a2i, b2r, b2i = e2
    return (a2r * a1r - a2i * a1i,
            a2r * a1i + a2i * a1r,
            a2r * b1r - a2i * b1i + b2r,
            a2r * b1i + a2i * b1r + b2i)


def ssm_branch(u, h0, lw):
    n, t, _ = u.shape
    f32 = jnp.float32
    a_re, a_im = lw['a_re'].astype(f32), lw['a_im'].astype(f32)
    dt = jnp.exp(lw['log_dt'].astype(f32))[:, None]
    mag = jnp.exp(a_re * dt)
    lam_r, lam_i = mag * jnp.cos(a_im * dt), mag * jnp.sin(a_im * dt)
    den = a_re * a_re + a_im * a_im
    zr = ((lam_r - 1.0) * a_re + lam_i * a_im) / den
    zi = (lam_i * a_re - (lam_r - 1.0) * a_im) / den
    b_re, b_im = lw['b_re'].astype(f32), lw['b_im'].astype(f32)
    bb_r = zr[..., None] * b_re - zi[..., None] * b_im
    bb_i = zr[..., None] * b_im + zi[..., None] * b_re
    ug = jnp.swapaxes(u.astype(f32).reshape(n, t, SSM_GROUPS, SSM_GROUP), 0, 1)
    x_r = jnp.einsum('tngh,gph->tngp', ug, bb_r)
    x_i = jnp.einsum('tngh,gph->tngp', ug, bb_i)
    h0r, h0i = h0[..., 0].astype(f32), h0[..., 1].astype(f32)
    x_r = x_r.at[0].add(lam_r * h0r - lam_i * h0i)
    x_i = x_i.at[0].add(lam_r * h0i + lam_i * h0r)
    shape = (t, 1, SSM_GROUPS, SSM_STATE)
    _, _, s_r, s_i = lax.associative_scan(
        cplx_affine_combine,
        (jnp.broadcast_to(lam_r, shape), jnp.broadcast_to(lam_i, shape), x_r, x_i), axis=0)
    y = (jnp.einsum('tngp,ghp->tngh', s_r, lw['c_re'].astype(f32))
         - jnp.einsum('tngp,ghp->tngh', s_i, lw['c_im'].astype(f32)))
    y = y + lw['d'].astype(f32).reshape(SSM_GROUPS, SSM_GROUP) * ug
    y = jnp.swapaxes(y, 0, 1).reshape(n, t, SSM_WIDTH)
    g = jax.nn.gelu(y)
    out = g * jax.nn.sigmoid(g @ lw['w_glu'].astype(f32) + lw['b_glu'].astype(f32))
    return out.astype(u.dtype), jnp.stack([s_r[-1], s_i[-1]], axis=-1)


def banded_attention(q, k, v, sink):
    n, l, hq, dh = q.shape
    hkv = k.shape[2]
    grp = hq // hkv
    nb = -(-l // BAND)
    padw = ((0, 0), (0, nb * BAND - l), (0, 0), (0, 0))
    q, k, v = jnp.pad(q, padw), jnp.pad(k, padw), jnp.pad(v, padw)
    qb = q.reshape(n, nb, BAND, hkv, grp, dh)

    def prev_and_cur(a):
        ab = a.reshape(n, nb, BAND, hkv, dh)
        prev = jnp.concatenate([jnp.zeros_like(ab[:, :1]), ab[:, :-1]], axis=1)
        return jnp.concatenate([prev, ab], axis=2)

    kb, vb = prev_and_cur(k), prev_and_cur(v)
    s = jnp.einsum('nbqhgd,nbkhd->nbhgqk', qb, kb, preferred_element_type=jnp.float32) * (dh ** -0.5)
    qi = jnp.arange(BAND)[:, None]
    kj = jnp.arange(2 * BAND)[None, :]
    dist = qi + BAND - kj
    kpos = (jnp.arange(nb)[:, None, None] - 1) * BAND + kj[None]
    mask = ((dist >= 0) & (dist <= BAND))[None] & (kpos >= 0)
    s = jnp.where(mask[None, :, None, None], s, -jnp.inf)
    lse = jax.nn.logsumexp(s, axis=-1)
    if sink is not None:
        lse = jnp.logaddexp(lse, sink.astype(jnp.float32).reshape(hkv, grp, 1))
    p = jnp.exp(s - lse[..., None]).astype(v.dtype)
    o = jnp.einsum('nbhgqk,nbkhd->nbqhgd', p, vb).reshape(n, nb * BAND, hq, dh)[:, :l]
    lse = lse.transpose(0, 1, 4, 2, 3).reshape(n, nb * BAND, hq)[:, :l]
    return o, lse


def gathered_attention(q, k_all, v_all, idx, valid, sink):
    n, t, hq, dh = q.shape
    hkv = k_all.shape[2]
    grp = hq // hkv
    kg, vg = k_all[:, idx], v_all[:, idx]
    qg = q.reshape(n, t, hkv, grp, dh)
    s = jnp.einsum('nthgd,ntkhd->nthgk', qg, kg, preferred_element_type=jnp.float32) * (dh ** -0.5)
    s = jnp.where(valid[None, :, None, None, :], s, -jnp.inf)
    lse = jax.nn.logsumexp(s, axis=-1)
    if sink is not None:
        lse = jnp.logaddexp(lse, sink.astype(jnp.float32).reshape(hkv, grp))
    p = jnp.exp(s - lse[..., None]).astype(v_all.dtype)
    o = jnp.einsum('nthgk,ntkhd->nthgd', p, vg).reshape(n, t, hq, dh)
    return o, lse.reshape(n, t, hq)


def window_index(window, dilation, buf_len, n_new):
    offs = jnp.arange(window // dilation + 1) * dilation
    idx = buf_len + jnp.arange(n_new)[:, None] - offs[None, :]
    return jnp.maximum(idx, 0), idx >= 0


def to_strided(a, d):
    n, l = a.shape[:2]
    rest = a.shape[2:]
    return jnp.swapaxes(a.reshape(n, l // d, d, *rest), 1, 2).reshape(n * d, l // d, *rest)


def from_strided(a, d, n):
    ld = a.shape[1]
    rest = a.shape[2:]
    return jnp.swapaxes(a.reshape(n, d, ld, *rest), 1, 2).reshape(n, ld * d, *rest)


def token_mixer(h, lw, cache):
    n, t, _ = h.shape
    z = jnp.einsum('btd,de->bte', h, lw['w_in'])
    offs = [int(o) for o in np.cumsum(SPLITS)[:-1]]
    pieces = jnp.split(z, offs, axis=-1)
    u = pieces[0]
    g_a, g_b, g_c = pieces[-N_BRANCH:]
    heads = lambda a: a.reshape(n, t, -1, HEAD_DIM)
    q_s, k_s, v_s = heads(pieces[1]), heads(pieces[2]), heads(pieces[3])

    h0 = jnp.zeros((n, SSM_GROUPS, SSM_STATE, 2), jnp.float32) if cache is None else cache['ssm']
    o_a, ssm_new = ssm_branch(u, h0, lw)

    if cache is None:
        o_b, _ = banded_attention(q_s, k_s, v_s, lw['sinks'])
        keep = min(SWA_WINDOW, t)
        swa_new = jnp.stack([k_s[:, t - keep:], v_s[:, t - keep:]], axis=2)
    else:
        buf = cache['swa']
        idx, valid = window_index(SWA_WINDOW, 1, buf.shape[1], t)
        o_b, _ = gathered_attention(q_s, jnp.concatenate([buf[:, :, 0], k_s], axis=1),
                                    jnp.concatenate([buf[:, :, 1], v_s], axis=1), idx, valid, lw['sinks'])
        swa_new = jnp.stack([k_s, v_s], axis=2)

    outs, lses, dil_new = [], [], []
    for gi, (win, dil) in enumerate(DIL_PAIRS):
        q_d, k_d, v_d = (heads(a) for a in pieces[4 + 3 * gi: 7 + 3 * gi])
        if cache is None:
            o, lse = banded_attention(to_strided(q_d, dil), to_strided(k_d, dil), to_strided(v_d, dil), None)
            o, lse = from_strided(o, dil, n), from_strided(lse, dil, n)
            keep = min(win, t)
            dil_new.append(jnp.stack([k_d[:, t - keep:], v_d[:, t - keep:]], axis=2))
        else:
            buf = cache['dil'][gi]
            idx, valid = window_index(win, dil, buf.shape[1], t)
            o, lse = gathered_attention(q_d, jnp.concatenate([buf[:, :, 0], k_d], axis=1),
                                        jnp.concatenate([buf[:, :, 1], v_d], axis=1), idx, valid, None)
            dil_new.append(jnp.stack([k_d, v_d], axis=2))
        outs.append(o)
        lses.append(lse)
    w = jax.nn.softmax(jnp.stack(lses), axis=0)[..., None]
    o_c = jnp.sum(w * jnp.stack(outs).astype(jnp.float32), axis=0).astype(h.dtype)

    y_a = o_a @ lw['w_branch_a']
    y_b = o_b.reshape(n, t, SWA_Q_W) @ lw['w_branch_b']
    y_c = o_c.reshape(n, t, DIL_W) @ lw['w_branch_c']
    merged = jax.nn.sigmoid(g_a) * y_a + jax.nn.sigmoid(g_b) * y_b + jax.nn.sigmoid(g_c) * y_c
    return merged @ lw['w_out'], (swa_new, dil_new[0], dil_new[1], dil_new[2], ssm_new)


def decoder_layer(x, p_l, lw, cache):
    mix, st = token_mixer(rmsnorm(x, lw['g_mix']), lw, cache)
    x = x + mix
    hm = rmsnorm(x, lw['g_mlp'])
    x = x + jnp.square(jax.nn.relu(hm @ lw['w_up'])) @ lw['w_down']
    gate = jax.nn.sigmoid(rmsnorm(x, lw['g_ple']) @ lw['w_ple_gate'])
    x = x + gate * (p_l @ lw['w_ple_proj'])
    return x, st


def setup_inputs(seed: int = 0) -> dict:
    key = jax.random.key(seed)
    ks = jax.random.split(key, 40)
    f32 = jnp.float32
    nrm = lambda i, shape, scale: scale * jax.random.normal(ks[i], shape, f32)
    L = DEPTH
    G, P, H = SSM_GROUPS, SSM_STATE, SSM_GROUP
    kv_shape = lambda win, nh: (L, DEC_BATCH, min(win, PAST_LEN), 2, nh, HEAD_DIM)
    return {
        'x_prompt': nrm(0, (BATCH, SEQ, D_MODEL), 1.0),
        'x_sample': nrm(1, (DEC_BATCH, DEC_SEQ, D_MODEL), 1.0),
        'cache_swa_kv': nrm(2, kv_shape(SWA_WINDOW, SWA_KV_HEADS), 1.0),
        'cache_dil_d1_kv': nrm(3, kv_shape(DIL_PAIRS[0][0], DIL_HEADS), 1.0),
        'cache_dil_d4_kv': nrm(4, kv_shape(DIL_PAIRS[1][0], DIL_HEADS), 1.0),
        'cache_dil_d16_kv': nrm(5, kv_shape(DIL_PAIRS[2][0], DIL_HEADS), 1.0),
        'state_ssm': nrm(6, (L, DEC_BATCH, G, P, 2), 0.1),
        'p_prompt': nrm(7, (L, BATCH, SEQ, PLE_DIM), 1.0),
        'p_sample': nrm(8, (L, DEC_BATCH, DEC_SEQ, PLE_DIM), 1.0),
        'w_in': nrm(9, (L, D_MODEL, D_IN), D_MODEL ** -0.5),
        'g_mix': 1.0 + nrm(10, (L, D_MODEL), 0.01),
        'ssm_a_re': -0.5 + nrm(11, (L, G, P), 0.01),
        'ssm_a_im': jnp.pi * jnp.arange(P, dtype=f32) + nrm(12, (L, G, P), 0.01),
        'ssm_log_dt': jax.random.uniform(ks[13], (L, G), f32, math.log(1e-3), math.log(1e-1)),
        'ssm_b_re': nrm(14, (L, G, P, H), (2 * H) ** -0.5),
        'ssm_b_im': nrm(15, (L, G, P, H), (2 * H) ** -0.5),
        'ssm_c_re': nrm(16, (L, G, H, P), P ** -0.5),
        'ssm_c_im': nrm(17, (L, G, H, P), P ** -0.5),
        'ssm_d': nrm(18, (L, SSM_WIDTH), 1.0),
        'w_glu': nrm(19, (L, SSM_WIDTH, SSM_WIDTH), SSM_WIDTH ** -0.5),
        'b_glu': nrm(20, (L, SSM_WIDTH), 0.01),
        'attn_sinks': nrm(21, (L, SWA_Q_HEADS), 0.5),
        'w_branch_a': nrm(22, (L, SSM_WIDTH, D_MODEL), SSM_WIDTH ** -0.5),
        'w_branch_b': nrm(23, (L, SWA_Q_W, D_MODEL), SWA_Q_W ** -0.5),
        'w_branch_c': nrm(24, (L, DIL_W, D_MODEL), DIL_W ** -0.5),
        'w_out': nrm(25, (L, D_MODEL, D_MODEL), D_MODEL ** -0.5),
        'g_mlp': 1.0 + nrm(26, (L, D_MODEL), 0.01),
        'w_up': nrm(27, (L, D_MODEL, D_FF), D_MODEL ** -0.5),
        'w_down': nrm(28, (L, D_FF, D_MODEL), D_FF ** -0.5),
        'g_ple': 1.0 + nrm(29, (L, D_MODEL), 0.01),
        'w_ple_gate': nrm(30, (L, D_MODEL, D_MODEL), D_MODEL ** -0.5),
        'w_ple_proj': nrm(31, (L, PLE_DIM, D_MODEL), PLE_DIM ** -0.5),
        'g_final': 1.0 + nrm(32, (D_MODEL,), 0.01),
    }


def reference(x_prompt, x_sample, cache_swa_kv, cache_dil_d1_kv, cache_dil_d4_kv, cache_dil_d16_kv, state_ssm,
              p_prompt, p_sample, w_in, g_mix, ssm_a_re, ssm_a_im, ssm_log_dt, ssm_b_re, ssm_b_im, ssm_c_re,
              ssm_c_im, ssm_d, w_glu, b_glu, attn_sinks, w_branch_a, w_branch_b, w_branch_c, w_out, g_mlp,
              w_up, w_down, g_ple, w_ple_gate, w_ple_proj, g_final):
    yp, ys = x_prompt, x_sample
    st_p, st_s = [], []
    for l in range(DEPTH):
        lw = {'w_in': w_in[l], 'g_mix': g_mix[l], 'a_re': ssm_a_re[l], 'a_im': ssm_a_im[l],
              'log_dt': ssm_log_dt[l], 'b_re': ssm_b_re[l], 'b_im': ssm_b_im[l], 'c_re': ssm_c_re[l],
              'c_im': ssm_c_im[l], 'd': ssm_d[l], 'w_glu': w_glu[l], 'b_glu': b_glu[l], 'sinks': attn_sinks[l],
              'w_branch_a': w_branch_a[l], 'w_branch_b': w_branch_b[l], 'w_branch_c': w_branch_c[l],
              'w_out': w_out[l], 'g_mlp': g_mlp[l], 'w_up': w_up[l], 'w_down': w_down[l], 'g_ple': g_ple[l],
              'w_ple_gate': w_ple_gate[l], 'w_ple_proj': w_ple_proj[l]}
        cache_l = {'swa': cache_swa_kv[l],
                   'dil': (cache_dil_d1_kv[l], cache_dil_d4_kv[l], cache_dil_d16_kv[l]),
                   'ssm': state_ssm[l]}
        yp, sp = decoder_layer(yp, p_prompt[l], lw, None)
        ys, ss = decoder_layer(ys, p_sample[l], lw, cache_l)
        st_p.append(sp)
        st_s.append(ss)
    yp = rmsnorm(yp, g_final)
    ys = rmsnorm(ys, g_final)
    stk = lambda sts, i: jnp.stack([s[i] for s in sts])
    return (yp, ys,
            stk(st_p, 0), stk(st_p, 1), stk(st_p, 2), stk(st_p, 3), stk(st_p, 4),
            stk(st_s, 0), stk(st_s, 1), stk(st_s, 2), stk(st_s, 3), stk(st_s, 4))
```

```python
import functools

import jax
import jax.numpy as jnp
from jax import lax
from jax.experimental import pallas as pl
from jax.experimental.pallas import tpu as pltpu

HEAD_DIM = 64
SSM_GROUP = 16
SSM_STATE = 64
SWA_Q_HEADS = 8
SWA_KV_HEADS = 2
DIL_HEADS = 4
DIL_PAIRS = ((128, 1), (512, 4), (2048, 16))
BAND = 128
SSM_CHUNK = 16
EPS = 1e-6
LANES = 128
NEG = -1e30
VMEM_LIMIT = 48 * 1024 * 1024
BF16 = jnp.bfloat16
F32 = jnp.float32


def _cparams(*sem):
    return pltpu.CompilerParams(dimension_semantics=sem, vmem_limit_bytes=VMEM_LIMIT)


def _rmsnorm_bf16(xf, g):
    ms = jnp.mean(xf * xf, axis=-1, keepdims=True)
    return (xf * lax.rsqrt(ms + EPS) * g).astype(BF16)


def _dot(a, b):
    return jnp.dot(a, b, preferred_element_type=F32)


def _row_tile(rows, pref):
    t = min(rows, pref)
    assert rows % t == 0, (rows, pref)
    return t


def _norm_matmul_kernel(x_ref, g_ref, w_ref, o_ref, hn_ref):
    @pl.when(pl.program_id(1) == 0)
    def _():
        hn_ref[...] = _rmsnorm_bf16(x_ref[...], g_ref[...])

    o_ref[...] = _dot(hn_ref[...], w_ref[...])


def norm_matmul(x, g, w, *, tm=1024, tn=512):
    rows, d = x.shape
    cols = w.shape[1]
    tm = _row_tile(rows, tm)
    assert cols % tn == 0
    return pl.pallas_call(
        _norm_matmul_kernel,
        grid=(rows // tm, cols // tn),
        in_specs=[pl.BlockSpec((tm, d), lambda i, j: (i, 0)),
                  pl.BlockSpec((1, d), lambda i, j: (0, 0)),
                  pl.BlockSpec((d, tn), lambda i, j: (0, j))],
        out_specs=pl.BlockSpec((tm, tn), lambda i, j: (i, j)),
        out_shape=jax.ShapeDtypeStruct((rows, cols), F32),
        scratch_shapes=[pltpu.VMEM((tm, d), BF16)],
        compiler_params=_cparams("parallel", "arbitrary"),
        name="norm_matmul",
    )(x, g, w)


def _ssm_discretize(lw):
    a_re, a_im = lw['a_re'].astype(F32), lw['a_im'].astype(F32)
    dt = jnp.exp(lw['log_dt'].astype(F32))[:, None]
    mag = jnp.exp(a_re * dt)
    lam_r, lam_i = mag * jnp.cos(a_im * dt), mag * jnp.sin(a_im * dt)
    den = a_re * a_re + a_im * a_im
    zr = ((lam_r - 1.0) * a_re + lam_i * a_im) / den
    zi = (lam_i * a_re - (lam_r - 1.0) * a_im) / den
    b_re, b_im = lw['b_re'].astype(F32), lw['b_im'].astype(F32)
    bb_r = zr[..., None] * b_re - zi[..., None] * b_im
    bb_i = zr[..., None] * b_im + zi[..., None] * b_re
    return lam_r, lam_i, bb_r, bb_i


def _ssm_chunk_operators(lw):
    lam_r, lam_i, bb_r, bb_i = _ssm_discretize(lw)
    c_re, c_im = lw['c_re'].astype(F32), lw['c_im'].astype(F32)
    L = SSM_CHUNK
    pw_r, pw_i = [jnp.ones_like(lam_r)], [jnp.zeros_like(lam_i)]
    for _ in range(L):
        pr, pi = pw_r[-1], pw_i[-1]
        pw_r.append(pr * lam_r - pi * lam_i)
        pw_i.append(pr * lam_i + pi * lam_r)
    pw_r, pw_i = jnp.stack(pw_r), jnp.stack(pw_i)
    cl_r = c_re[None] * pw_r[:, :, None, :] - c_im[None] * pw_i[:, :, None, :]
    cl_i = c_re[None] * pw_i[:, :, None, :] + c_im[None] * pw_r[:, :, None, :]
    hi = lax.Precision.HIGHEST
    kj = (jnp.einsum('jgop,gpi->jgoi', cl_r[:L], bb_r, precision=hi)
          - jnp.einsum('jgop,gpi->jgoi', cl_i[:L], bb_i, precision=hi))
    tsrc = jnp.arange(L)[:, None]
    tdst = jnp.arange(L)[None, :]
    lag = tdst - tsrc
    m = jnp.where((lag >= 0)[:, :, None, None, None], kj[jnp.clip(lag, 0, L - 1)], 0.0)
    g, h, p = bb_r.shape[0], bb_r.shape[2], bb_r.shape[1]
    m = m.transpose(2, 0, 4, 1, 3).reshape(g, L * h, L * h)
    rev_r, rev_i = pw_r[L - 1::-1][:L], pw_i[L - 1::-1][:L]
    p_re = rev_r[:, :, :, None] * bb_r[None] - rev_i[:, :, :, None] * bb_i[None]
    p_im = rev_r[:, :, :, None] * bb_i[None] + rev_i[:, :, :, None] * bb_r[None]
    p_re = p_re.transpose(1, 0, 3, 2).reshape(g, L * h, p)
    p_im = p_im.transpose(1, 0, 3, 2).reshape(g, L * h, p)
    q_re = cl_r[1:].transpose(1, 3, 0, 2).reshape(g, p, L * h)
    q_im = (-cl_i[1:]).transpose(1, 3, 0, 2).reshape(g, p, L * h)
    return m, p_re, p_im, q_re, q_im, pw_r[L], pw_i[L]


def _ssm_chunk_kernel(u_ref, m_ref, pre_ref, pim_ref, qre_ref, qim_ref, lr_ref, li_ref, d_ref,
                      y_ref, fr_ref, fi_ref, xre, xim, sre, sim, *, nb):
    u = u_ref[...]
    ub = u.astype(BF16)
    xre[...] = _dot(ub, pre_ref[...])
    xim[...] = _dot(ub, pim_ref[...])
    rows, p = xre.shape
    lr = jnp.broadcast_to(lr_ref[...], (nb, p))
    li = jnp.broadcast_to(li_ref[...], (nb, p))

    def step(c, carry):
        sr, si = carry
        r0 = pl.multiple_of(c * nb, nb)
        sre[pl.ds(r0, nb), :] = sr
        sim[pl.ds(r0, nb), :] = si
        return (lr * sr - li * si + xre[pl.ds(r0, nb), :],
                lr * si + li * sr + xim[pl.ds(r0, nb), :])

    zero = jnp.zeros((nb, p), F32)
    sr, si = lax.fori_loop(0, rows // nb, step, (zero, zero), unroll=8)
    fr_ref[...] = sr
    fi_ref[...] = si
    y_ref[...] = (_dot(ub, m_ref[...]) + _dot(sre[...].astype(BF16), qre_ref[...])
                  + _dot(sim[...].astype(BF16), qim_ref[...]) + u * d_ref[...])


def ssm_chunk(u_g, ops, d_tile, *, nb):
    m, p_re, p_im, q_re, q_im, l_r, l_i = ops
    g, rows, w = u_g.shape
    p = p_re.shape[-1]
    assert nb % 8 == 0 and rows % nb == 0
    per_g = lambda *shape: pl.BlockSpec((None,) + shape, lambda i: (i,) + (0,) * len(shape))
    return pl.pallas_call(
        functools.partial(_ssm_chunk_kernel, nb=nb),
        grid=(g,),
        in_specs=[per_g(rows, w), per_g(w, w), per_g(w, p), per_g(w, p), per_g(p, w), per_g(p, w),
                  per_g(1, p), per_g(1, p), per_g(1, w)],
        out_specs=[per_g(rows, w), per_g(nb, p), per_g(nb, p)],
        out_shape=[jax.ShapeDtypeStruct((g, rows, w), F32),
                   jax.ShapeDtypeStruct((g, nb, p), F32),
                   jax.ShapeDtypeStruct((g, nb, p), F32)],
        scratch_shapes=[pltpu.VMEM((rows, p), F32)] * 4,
        compiler_params=_cparams("parallel"),
        name="ssm_chunk",
    )(u_g, m.astype(BF16), p_re.astype(BF16), p_im.astype(BF16), q_re.astype(BF16), q_im.astype(BF16),
      l_r[:, None, :], l_i[:, None, :], d_tile)


def _ssm_step_kernel(u_ref, h0r_ref, h0i_ref, bbr_ref, bbi_ref, ccr_ref, cci_ref, lr_ref, li_ref, d_ref,
                     y_ref, sr_ref, si_ref):
    u = u_ref[...]
    ub = u.astype(BF16)
    lr, li = lr_ref[...], li_ref[...]
    h0r, h0i = h0r_ref[...], h0i_ref[...]
    sr = _dot(ub, bbr_ref[...]) + (lr * h0r - li * h0i)
    si = _dot(ub, bbi_ref[...]) + (lr * h0i + li * h0r)
    sr_ref[...] = sr
    si_ref[...] = si
    y_ref[...] = _dot(sr.astype(BF16), ccr_ref[...]) - _dot(si.astype(BF16), cci_ref[...]) + u * d_ref[...]


def ssm_step(u, h0r, h0i, lw):
    lam_r, lam_i, bb_r, bb_i = _ssm_discretize(lw)
    g, p, h = bb_r.shape
    eye = jnp.eye(g, dtype=F32)
    dense_in = lambda b: (b.transpose(0, 2, 1)[:, :, None, :] * eye[:, None, :, None]).reshape(g * h, g * p)
    dense_out = lambda c: (c.transpose(0, 2, 1)[:, :, None, :] * eye[:, None, :, None]).reshape(g * p, g * h)
    n = u.shape[0]
    args = (u, h0r, h0i, dense_in(bb_r).astype(BF16), dense_in(bb_i).astype(BF16),
            dense_out(lw['c_re'].astype(F32)).astype(BF16), dense_out(lw['c_im'].astype(F32)).astype(BF16),
            lam_r.reshape(1, g * p), lam_i.reshape(1, g * p), lw['d'].astype(F32).reshape(1, g * h))
    full = lambda a: pl.BlockSpec(a.shape, lambda i: (0, 0))
    return pl.pallas_call(
        _ssm_step_kernel,
        grid=(1,),
        in_specs=[full(a) for a in args],
        out_specs=[pl.BlockSpec((n, g * h), lambda i: (0, 0)),
                   pl.BlockSpec((n, g * p), lambda i: (0, 0)),
                   pl.BlockSpec((n, g * p), lambda i: (0, 0))],
        out_shape=[jax.ShapeDtypeStruct((n, g * h), F32),
                   jax.ShapeDtypeStruct((n, g * p), F32),
                   jax.ShapeDtypeStruct((n, g * p), F32)],
        compiler_params=_cparams("arbitrary"),
        name="ssm_step",
    )(*args)


def _band_attn_kernel(*refs, n_pairs, heads, with_sink, with_lse):
    it = iter(refs)
    q_ref, kp_ref, kc_ref, vp_ref, vc_ref = (next(it) for _ in range(5))
    sink_ref = next(it) if with_sink else None
    o_ref = next(it)
    lse_ref = next(it) if with_lse else None
    b = pl.program_id(1)
    k = jnp.concatenate([kp_ref[...], kc_ref[...]], axis=0)
    v = jnp.concatenate([vp_ref[...], vc_ref[...]], axis=0)
    lane = lax.broadcasted_iota(jnp.int32, (1, LANES), 1)
    half_mask = [(lane < HEAD_DIM).astype(F32), (lane >= HEAD_DIM).astype(F32)]
    qi = lax.broadcasted_iota(jnp.int32, (BAND, 2 * BAND), 0)
    kj = lax.broadcasted_iota(jnp.int32, (BAND, 2 * BAND), 1)
    mask = (kj >= qi) & (kj <= qi + BAND) & ((kj >= BAND) | (b > 0))
    kv_cache = {}

    def kv_variant(pair, rot, half):
        key = (pair, rot, half)
        if key not in kv_cache:
            k2 = k[:, pair * LANES:(pair + 1) * LANES]
            v2 = v[:, pair * LANES:(pair + 1) * LANES]
            if rot:
                k2 = pltpu.roll(k2, HEAD_DIM, 1)
                v2 = pltpu.roll(v2, HEAD_DIM, 1)
            kv_cache[key] = (k2.astype(BF16), (v2 * half_mask[half]).astype(BF16))
        return kv_cache[key]

    for j in range(n_pairs):
        q2 = q_ref[:, j * LANES:(j + 1) * LANES] * (HEAD_DIM ** -0.5)
        o2 = jnp.zeros((BAND, LANES), F32)
        lse2 = jnp.zeros((BAND, LANES), F32)
        for half in (0, 1):
            pair, rot = heads[j][half]
            kb, vb = kv_variant(pair, rot, half)
            qm = (q2 * half_mask[half]).astype(BF16)
            s = lax.dot_general(qm, kb, (((1,), (1,)), ((), ())), preferred_element_type=F32)
            s = jnp.where(mask, s, NEG)
            m = jnp.max(s, axis=-1, keepdims=True)
            e = jnp.exp(s - m)
            lse = m + jnp.log(jnp.sum(e, axis=-1, keepdims=True))
            if with_sink:
                sk = sink_ref[2 * j + half]
                mx = jnp.maximum(lse, sk)
                lse = mx + jnp.log(jnp.exp(lse - mx) + jnp.exp(sk - mx))
            p = e * jnp.exp(m - lse)
            o2 = o2 + _dot(p.astype(BF16), vb)
            if with_lse:
                lse2 = lse2 + lse * half_mask[half]
        o_ref[:, j * LANES:(j + 1) * LANES] = o2
        if with_lse:
            lse_ref[:, j * LANES:(j + 1) * LANES] = lse2


def band_attn(src, *, q_col, k_col, v_col, q_heads, kv_heads, sinks=None, with_lse=False):
    s_, l_, _ = src.shape
    assert l_ % BAND == 0 and q_heads % 2 == 0 and kv_heads % 2 == 0
    grp = q_heads // kv_heads
    n_pairs = q_heads // 2
    wq, wkv = q_heads * HEAD_DIM, kv_heads * HEAD_DIM
    assert q_col % wq == 0 and k_col % wkv == 0 and v_col % wkv == 0
    heads = []
    for j in range(n_pairs):
        pair_heads = []
        for half in (0, 1):
            kvh = (2 * j + half) // grp
            pair_heads.append((kvh // 2, (kvh % 2) != half))
        heads.append(tuple(pair_heads))
    cur = lambda col, w: pl.BlockSpec((None, BAND, w), lambda n, b: (n, b, col // w))
    prev = lambda col, w: pl.BlockSpec((None, BAND, w), lambda n, b: (n, jnp.maximum(b - 1, 0), col // w))
    in_specs = [cur(q_col, wq), prev(k_col, wkv), cur(k_col, wkv), prev(v_col, wkv), cur(v_col, wkv)]
    args = [src] * 5
    if sinks is not None:
        in_specs.append(pl.BlockSpec(memory_space=pltpu.SMEM))
        args.append(sinks.astype(F32))
    out_spec = pl.BlockSpec((None, BAND, wq), lambda n, b: (n, b, 0))
    out_sds = jax.ShapeDtypeStruct((s_, l_, wq), F32)
    return pl.pallas_call(
        functools.partial(_band_attn_kernel, n_pairs=n_pairs, heads=tuple(heads),
                          with_sink=sinks is not None, with_lse=with_lse),
        grid=(s_, l_ // BAND),
        in_specs=in_specs,
        out_specs=[out_spec, out_spec] if with_lse else out_spec,
        out_shape=[out_sds, out_sds] if with_lse else out_sds,
        compiler_params=_cparams("parallel", "arbitrary"),
        name="band_attn",
    )(*args)


def _decode_attn_kernel(*refs, grp, with_sink):
    it = iter(refs)
    q_ref, kv_ref, new_ref = next(it), next(it), next(it)
    sink_ref = next(it) if with_sink else None
    o_ref, lse_ref = next(it), next(it)
    k, v = kv_ref[:, 0], kv_ref[:, 1]
    k_new, v_new = new_ref[0], new_ref[1]
    scale = HEAD_DIM ** -0.5
    for g in range(grp):
        q = q_ref[g]
        s = jnp.sum(k * q[None], axis=-1, keepdims=True) * scale
        s_new = jnp.sum(k_new * q, axis=-1, keepdims=True) * scale
        m = jnp.maximum(jnp.max(s, axis=0), s_new)
        lse = m + jnp.log(jnp.sum(jnp.exp(s - m[None]), axis=0) + jnp.exp(s_new - m))
        if with_sink:
            sk = sink_ref[g][:, :1]
            mx = jnp.maximum(lse, sk)
            lse = mx + jnp.log(jnp.exp(lse - mx) + jnp.exp(sk - mx))
        p = jnp.exp(s - lse[None])
        o_ref[g] = jnp.sum(p * v, axis=0) + jnp.exp(s_new - lse) * v_new
        lse_ref[g] = jnp.broadcast_to(lse, (lse.shape[0], HEAD_DIM))


def decode_attn(q, cache, layer, new_kv, *, dilation, sinks=None):
    n, grp, hkv, dh = q.shape
    depth, _, length = cache.shape[:3]
    rows = length // dilation
    cache = cache.reshape(depth, n, rows, dilation, 2, hkv, dh)
    in_specs = [pl.BlockSpec((None, grp, hkv, dh), lambda i: (i, 0, 0, 0)),
                pl.BlockSpec((None, None, rows, None, 2, hkv, dh), lambda i: (layer, i, 0, 0, 0, 0, 0)),
                pl.BlockSpec((None, 2, hkv, dh), lambda i: (i, 0, 0, 0))]
    args = [q, cache, new_kv]
    if sinks is not None:
        in_specs.append(pl.BlockSpec((grp, hkv, dh), lambda i: (0, 0, 0)))
        args.append(sinks)
    out_spec = pl.BlockSpec((None, grp, hkv, dh), lambda i: (i, 0, 0, 0))
    out_sds = jax.ShapeDtypeStruct((n, grp, hkv, dh), F32)
    return pl.pallas_call(
        functools.partial(_decode_attn_kernel, grp=grp, with_sink=sinks is not None),
        grid=(n,),
        in_specs=in_specs,
        out_specs=[out_spec, out_spec],
        out_shape=[out_sds, out_sds],
        compiler_params=_cparams("parallel"),
        name="decode_attn",
    )(*args)


def _merge_kernel(x_ref, ya_ref, ob_ref, oc0_ref, oc1_ref, oc2_ref, l0_ref, l1_ref, l2_ref,
                  ga0_ref, ga1_ref, gb0_ref, gb1_ref, gc0_ref, gc1_ref,
                  wglu_ref, bglu_ref, wba_ref, wbb_ref, wbc_ref, wout_ref, o_ref):
    g = jax.nn.gelu(ya_ref[...])
    oa = g * jax.nn.sigmoid(_dot(g.astype(BF16), wglu_ref[...]) + bglu_ref[...])
    l0, l1, l2 = l0_ref[...], l1_ref[...], l2_ref[...]
    mx = jnp.maximum(jnp.maximum(l0, l1), l2)
    e0, e1, e2 = jnp.exp(l0 - mx), jnp.exp(l1 - mx), jnp.exp(l2 - mx)
    oc = (e0 * oc0_ref[...] + e1 * oc1_ref[...] + e2 * oc2_ref[...]) / (e0 + e1 + e2)
    gate = lambda r0, r1: jax.nn.sigmoid(jnp.concatenate([r0[...], r1[...]], axis=-1))
    merged = (gate(ga0_ref, ga1_ref) * _dot(oa.astype(BF16), wba_ref[...])
              + gate(gb0_ref, gb1_ref) * _dot(ob_ref[...].astype(BF16), wbb_ref[...])
              + gate(gc0_ref, gc1_ref) * _dot(oc.astype(BF16), wbc_ref[...]))
    o_ref[...] = x_ref[...] + _dot(merged.astype(BF16), wout_ref[...])


def merge(x, z, gate_col, ya, ob, ocs, lses, lw, *, tm=512):
    rows, d = x.shape
    tm = _row_tile(rows, tm)
    half = d // 2
    assert gate_col % half == 0
    row = lambda w: pl.BlockSpec((tm, w), lambda i: (i, 0))
    gate = lambda k: pl.BlockSpec((tm, half), lambda i: (i, gate_col // half + k))
    full = lambda a: pl.BlockSpec(a.shape, lambda i: (0, 0))
    weights = (lw['w_glu'].astype(BF16), lw['b_glu'].astype(F32).reshape(1, -1), lw['w_branch_a'].astype(BF16),
               lw['w_branch_b'].astype(BF16), lw['w_branch_c'].astype(BF16), lw['w_out'].astype(BF16))
    wc = ocs[0].shape[1]
    return pl.pallas_call(
        _merge_kernel,
        grid=(rows // tm,),
        in_specs=([row(d), row(ya.shape[1]), row(ob.shape[1])] + [row(wc)] * 6 + [gate(k) for k in range(6)]
                  + [full(w) for w in weights]),
        out_specs=row(d),
        out_shape=jax.ShapeDtypeStruct((rows, d), F32),
        compiler_params=_cparams("parallel"),
        name="merge",
    )(x, ya, ob, *ocs, *lses, *([z] * 6), *weights)


def _mlp_kernel(x_ref, g_ref, wup_ref, wdn_ref, o_ref, hn_ref, acc_ref):
    k = pl.program_id(1)

    @pl.when(k == 0)
    def _():
        hn_ref[...] = _rmsnorm_bf16(x_ref[...], g_ref[...])
        acc_ref[...] = jnp.zeros_like(acc_ref)

    h = jnp.maximum(_dot(hn_ref[...], wup_ref[...]), 0.0)
    acc_ref[...] += _dot((h * h).astype(BF16), wdn_ref[...])

    @pl.when(k == pl.num_programs(1) - 1)
    def _():
        o_ref[...] = x_ref[...] + acc_ref[...]


def mlp(x, g, w_up, w_down, *, tm=1024, tk=512):
    rows, d = x.shape
    dff = w_up.shape[1]
    tm = _row_tile(rows, tm)
    assert dff % tk == 0
    return pl.pallas_call(
        _mlp_kernel,
        grid=(rows // tm, dff // tk),
        in_specs=[pl.BlockSpec((tm, d), lambda i, k: (i, 0)),
                  pl.BlockSpec((1, d), lambda i, k: (0, 0)),
                  pl.BlockSpec((d, tk), lambda i, k: (0, k)),
                  pl.BlockSpec((tk, d), lambda i, k: (k, 0))],
        out_specs=pl.BlockSpec((tm, d), lambda i, k: (i, 0)),
        out_shape=jax.ShapeDtypeStruct((rows, d), F32),
        scratch_shapes=[pltpu.VMEM((tm, d), BF16), pltpu.VMEM((tm, d), F32)],
        compiler_params=_cparams("parallel", "arbitrary"),
        name="mlp",
    )(x, g, w_up, w_down)


def _ple_kernel(x_ref, p_ref, g_ref, wg_ref, wp_ref, gf_ref, o_ref, *, final_norm):
    x = x_ref[...]
    gate = jax.nn.sigmoid(_dot(_rmsnorm_bf16(x, g_ref[...]), wg_ref[...]))
    y = x + gate * _dot(p_ref[...].astype(BF16), wp_ref[...])
    if final_norm:
        ms = jnp.mean(y * y, axis=-1, keepdims=True)
        y = y * lax.rsqrt(ms + EPS) * gf_ref[...]
    o_ref[...] = y


def ple(x, p, layer, g, w_gate, w_proj, g_final, *, final_norm, tm=512):
    rows, d = x.shape
    tm = _row_tile(rows, tm)
    row = lambda w: pl.BlockSpec((tm, w), lambda i: (i, 0))
    full = lambda a: pl.BlockSpec(a.shape, lambda i: (0, 0))
    return pl.pallas_call(
        functools.partial(_ple_kernel, final_norm=final_norm),
        grid=(rows // tm,),
        in_specs=[row(d), pl.BlockSpec((None, tm, p.shape[2]), lambda i: (layer, i, 0)),
                  full(g), full(w_gate), full(w_proj), full(g_final)],
        out_specs=row(d),
        out_shape=jax.ShapeDtypeStruct((rows, d), F32),
        compiler_params=_cparams("parallel"),
        name="ple",
    )(x, p, g, w_gate, w_proj, g_final)


def _column_offsets():
    ssm_w = 32 * SSM_GROUP
    swa_q, swa_kv, dil_w = SWA_Q_HEADS * HEAD_DIM, SWA_KV_HEADS * HEAD_DIM, DIL_HEADS * HEAD_DIM
    off = {'u': 0, 'swa_q': ssm_w, 'swa_k': ssm_w + swa_q, 'swa_v': ssm_w + swa_q + swa_kv}
    off['dil'] = ssm_w + swa_q + 2 * swa_kv
    off['gates'] = off['dil'] + 3 * len(DIL_PAIRS) * dil_w
    return off


def _prompt_mixer(z, n, t, lw):
    off = _column_offsets()
    rows = n * t
    z3 = z.reshape(n, t, -1)
    dil_w = DIL_HEADS * HEAD_DIM
    ssm_w = off['swa_q']
    groups = ssm_w // SSM_GROUP

    chunks = t // SSM_CHUNK
    u_g = (z3[:, :, :ssm_w].reshape(n, chunks, SSM_CHUNK, groups, SSM_GROUP)
           .transpose(3, 1, 0, 2, 4).reshape(groups, chunks * n, SSM_CHUNK * SSM_GROUP))
    d_tile = jnp.tile(lw['d'].astype(F32).reshape(groups, 1, SSM_GROUP), (1, 1, SSM_CHUNK))
    y_g, fin_r, fin_i = ssm_chunk(u_g, _ssm_chunk_operators(lw), d_tile, nb=n)
    ya = (y_g.reshape(groups, chunks, n, SSM_CHUNK, SSM_GROUP)
          .transpose(2, 1, 3, 0, 4).reshape(rows, ssm_w))
    ssm_new = jnp.stack([fin_r.transpose(1, 0, 2), fin_i.transpose(1, 0, 2)], axis=-1)

    ob = band_attn(z3, q_col=off['swa_q'], k_col=off['swa_k'], v_col=off['swa_v'],
                   q_heads=SWA_Q_HEADS, kv_heads=SWA_KV_HEADS, sinks=lw['sinks']).reshape(rows, -1)
    keep = min(BAND, t)
    swa_new = z3[:, t - keep:, off['swa_k']:off['swa_k'] + 2 * SWA_KV_HEADS * HEAD_DIM].reshape(
        n, keep, 2, SWA_KV_HEADS, HEAD_DIM)

    ocs, lses, dil_new = [], [], []
    for gi, (win, dil) in enumerate(DIL_PAIRS):
        c0 = off['dil'] + 3 * dil_w * gi
        if dil == 1:
            o, lse = band_attn(z3, q_col=c0, k_col=c0 + dil_w, v_col=c0 + 2 * dil_w,
                               q_heads=DIL_HEADS, kv_heads=DIL_HEADS, with_lse=True)
        else:
            qkv = (z3[:, :, c0:c0 + 3 * dil_w].reshape(n, t // dil, dil, 3 * dil_w)
                   .swapaxes(1, 2).reshape(n * dil, t // dil, 3 * dil_w))
            o, lse = band_attn(qkv, q_col=0, k_col=dil_w, v_col=2 * dil_w,
                               q_heads=DIL_HEADS, kv_heads=DIL_HEADS, with_lse=True)
            unstride = lambda a: a.reshape(n, dil, t // dil, dil_w).swapaxes(1, 2)
            o, lse = unstride(o), unstride(lse)
        ocs.append(o.reshape(rows, dil_w))
        lses.append(lse.reshape(rows, dil_w))
        keep = min(win, t)
        dil_new.append(z3[:, t - keep:, c0 + dil_w:c0 + 3 * dil_w].reshape(n, keep, 2, DIL_HEADS, HEAD_DIM))
    return ya, ob, ocs, lses, (swa_new, dil_new[0], dil_new[1], dil_new[2], ssm_new)


def _sample_mixer(z, lw, cache):
    off = _column_offsets()
    n = z.shape[0]
    dil_w = DIL_HEADS * HEAD_DIM
    ssm_w = off['swa_q']

    h0 = cache['ssm'].astype(F32)
    ya, s_r, s_i = ssm_step(z[:, :ssm_w], h0[..., 0].reshape(n, -1), h0[..., 1].reshape(n, -1), lw)
    ssm_new = jnp.stack([s_r.reshape(h0.shape[:-1]), s_i.reshape(h0.shape[:-1])], axis=-1)

    grp = SWA_Q_HEADS // SWA_KV_HEADS
    q = z[:, off['swa_q']:off['swa_k']].reshape(n, SWA_KV_HEADS, grp, HEAD_DIM).swapaxes(1, 2)
    new_kv = z[:, off['swa_k']:off['swa_k'] + 2 * SWA_KV_HEADS * HEAD_DIM].reshape(n, 2, SWA_KV_HEADS, HEAD_DIM)
    sinks = jnp.broadcast_to(lw['sinks'].astype(F32).reshape(SWA_KV_HEADS, grp).T[:, :, None],
                             (grp, SWA_KV_HEADS, HEAD_DIM))
    o, _ = decode_attn(q, cache['swa'], cache['layer'], new_kv, dilation=1, sinks=sinks)
    ob = o.swapaxes(1, 2).reshape(n, SWA_Q_HEADS * HEAD_DIM)
    swa_new = new_kv[:, None]

    ocs, lses, dil_new = [], [], []
    for gi, (win, dil) in enumerate(DIL_PAIRS):
        c0 = off['dil'] + 3 * dil_w * gi
        q = z[:, c0:c0 + dil_w].reshape(n, 1, DIL_HEADS, HEAD_DIM)
        new_kv = z[:, c0 + dil_w:c0 + 3 * dil_w].reshape(n, 2, DIL_HEADS, HEAD_DIM)
        o, lse = decode_attn(q, cache['dil'][gi], cache['layer'], new_kv, dilation=dil)
        ocs.append(o.reshape(n, dil_w))
        lses.append(lse.reshape(n, dil_w))
        dil_new.append(new_kv[:, None])
    return ya, ob, ocs, lses, (swa_new, dil_new[0], dil_new[1], dil_new[2], ssm_new)


def _layer(x, p, layer, lw, cache, n, t, g_final, final_norm):
    off = _column_offsets()
    row1 = lambda a: a.astype(F32).reshape(1, -1)
    z = norm_matmul(x, row1(lw['g_mix']), lw['w_in'].astype(BF16))
    if cache is None:
        ya, ob, ocs, lses, st = _prompt_mixer(z, n, t, lw)
    else:
        assert t == 1
        win_rows = [c.shape[2] for c in (cache['swa'],) + tuple(cache['dil'])]
        assert win_rows == [BAND] + [w for w, _ in DIL_PAIRS], win_rows
        ya, ob, ocs, lses, st = _sample_mixer(z, lw, cache)
    x = merge(x, z, off['gates'], ya, ob, ocs, lses, lw)
    x = mlp(x, row1(lw['g_mlp']), lw['w_up'].astype(BF16), lw['w_down'].astype(BF16))
    x = ple(x, p, layer, row1(lw['g_ple']), lw['w_ple_gate'].astype(BF16), lw['w_ple_proj'].astype(BF16),
            row1(g_final), final_norm=final_norm)
    return x, st


def kernel(x_prompt, x_sample, cache_swa_kv, cache_dil_d1_kv, cache_dil_d4_kv, cache_dil_d16_kv, state_ssm, p_prompt, p_sample, w_in, g_mix, ssm_a_re, ssm_a_im, ssm_log_dt, ssm_b_re, ssm_b_im, ssm_c_re, ssm_c_im, ssm_d, w_glu, b_glu, attn_sinks, w_branch_a, w_branch_b, w_branch_c, w_out, g_mlp, w_up, w_down, g_ple, w_ple_gate, w_ple_proj, g_final):
    depth = w_in.shape[0]
    bp, tp, d = x_prompt.shape
    bs, ts, _ = x_sample.shape
    yp = x_prompt.reshape(bp * tp, d)
    ys = x_sample.reshape(bs * ts, d)
    st_p, st_s = [], []
    for l in range(depth):
        lw = {'w_in': w_in[l], 'g_mix': g_mix[l], 'a_re': ssm_a_re[l], 'a_im': ssm_a_im[l],
              'log_dt': ssm_log_dt[l], 'b_re': ssm_b_re[l], 'b_im': ssm_b_im[l], 'c_re': ssm_c_re[l],
              'c_im': ssm_c_im[l], 'd': ssm_d[l], 'w_glu': w_glu[l], 'b_glu': b_glu[l], 'sinks': attn_sinks[l],
              'w_branch_a': w_branch_a[l], 'w_branch_b': w_branch_b[l], 'w_branch_c': w_branch_c[l],
              'w_out': w_out[l], 'g_mlp': g_mlp[l], 'w_up': w_up[l], 'w_down': w_down[l], 'g_ple': g_ple[l],
              'w_ple_gate': w_ple_gate[l], 'w_ple_proj': w_ple_proj[l]}
        cache_l = {'layer': l, 'swa': cache_swa_kv,
                   'dil': (cache_dil_d1_kv, cache_dil_d4_kv, cache_dil_d16_kv),
                   'ssm': state_ssm[l]}
        last = l == depth - 1
        yp, sp = _layer(yp, p_prompt.reshape(depth, bp * tp, -1), l, lw, None, bp, tp, g_final, last)
        ys, ss = _layer(ys, p_sample.reshape(depth, bs * ts, -1), l, lw, cache_l, bs, ts, g_final, last)
        st_p.append(sp)
        st_s.append(ss)
    stk = lambda sts, i: jnp.stack([s[i] for s in sts])
    return (yp.reshape(bp, tp, d), ys.reshape(bs, ts, d),
            stk(st_p, 0), stk(st_p, 1), stk(st_p, 2), stk(st_p, 3), stk(st_p, 4),
            stk(st_s, 0), stk(st_s, 1), stk(st_s, 2), stk(st_s, 3), stk(st_s, 4))
```

```python
import functools

import jax
import jax.numpy as jnp
from jax import lax
from jax.experimental import pallas as pl
from jax.experimental.pallas import tpu as pltpu

HEAD_DIM = 64
SSM_GROUP = 16
SSM_STATE = 64
SWA_Q_HEADS = 8
SWA_KV_HEADS = 2
DIL_HEADS = 4
DIL_PAIRS = ((128, 1), (512, 4), (2048, 16))
BAND = 128
SSM_CHUNK = 16
SSM_TILE_GROUPS = 16
EPS = 1e-6
LANES = 128
NEG = -1e30
VMEM_LIMIT = 48 * 1024 * 1024
BF16 = jnp.bfloat16
F32 = jnp.float32


def _cparams(*sem):
    return pltpu.CompilerParams(dimension_semantics=sem, vmem_limit_bytes=VMEM_LIMIT)


def _rmsnorm_bf16(xf, g):
    ms = jnp.mean(xf * xf, axis=-1, keepdims=True)
    return (xf * lax.rsqrt(ms + EPS) * g).astype(BF16)


def _dot(a, b):
    return jnp.dot(a, b, preferred_element_type=F32)


def _row_tile(rows, pref):
    t = min(rows, pref)
    assert rows % t == 0, (rows, pref)
    return t


def _norm_matmul_kernel(x_ref, g_ref, w_ref, o_ref, hn_ref):
    @pl.when(pl.program_id(1) == 0)
    def _():
        hn_ref[...] = _rmsnorm_bf16(x_ref[...], g_ref[...])

    o_ref[...] = _dot(hn_ref[...], w_ref[...])


def norm_matmul(x, g, w, *, tm=1024, tn=512):
    rows, d = x.shape
    cols = w.shape[1]
    tm = _row_tile(rows, tm)
    assert cols % tn == 0
    return pl.pallas_call(
        _norm_matmul_kernel,
        grid=(rows // tm, cols // tn),
        in_specs=[pl.BlockSpec((tm, d), lambda i, j: (i, 0)),
                  pl.BlockSpec((1, d), lambda i, j: (0, 0)),
                  pl.BlockSpec((d, tn), lambda i, j: (0, j))],
        out_specs=pl.BlockSpec((tm, tn), lambda i, j: (i, j)),
        out_shape=jax.ShapeDtypeStruct((rows, cols), F32),
        scratch_shapes=[pltpu.VMEM((tm, d), BF16)],
        compiler_params=_cparams("parallel", "arbitrary"),
        name="norm_matmul",
    )(x, g, w)


def _ssm_discretize(lw):
    a_re, a_im = lw['a_re'].astype(F32), lw['a_im'].astype(F32)
    dt = jnp.exp(lw['log_dt'].astype(F32))[:, None]
    mag = jnp.exp(a_re * dt)
    lam_r, lam_i = mag * jnp.cos(a_im * dt), mag * jnp.sin(a_im * dt)
    den = a_re * a_re + a_im * a_im
    zr = ((lam_r - 1.0) * a_re + lam_i * a_im) / den
    zi = (lam_i * a_re - (lam_r - 1.0) * a_im) / den
    b_re, b_im = lw['b_re'].astype(F32), lw['b_im'].astype(F32)
    bb_r = zr[..., None] * b_re - zi[..., None] * b_im
    bb_i = zr[..., None] * b_im + zi[..., None] * b_re
    return lam_r, lam_i, bb_r, bb_i


def _ssm_scan_operands(lw):
    lam_r, lam_i, bb_r, bb_i = _ssm_discretize(lw)
    g, p, h = bb_r.shape
    tg = SSM_TILE_GROUPS
    eye = jnp.eye(tg, dtype=F32)
    tile_in = lambda b: (b.reshape(g // tg, tg, p, h).transpose(0, 1, 3, 2)[:, :, :, None, :]
                         * eye[None, :, None, :, None]).reshape(g // tg, tg * h, tg * p)
    tile_out = lambda c: (c.reshape(g // tg, tg, h, p).transpose(0, 1, 3, 2)[:, :, :, None, :]
                          * eye[None, :, None, :, None]).reshape(g // tg, tg * p, tg * h)
    pw_r, pw_i = [lam_r], [lam_i]
    for _ in range(SSM_CHUNK - 1):
        pr, pi = pw_r[-1], pw_i[-1]
        pw_r.append(pr * lam_r - pi * lam_i)
        pw_i.append(pr * lam_i + pi * lam_r)
    pw = jnp.stack([jnp.stack(pw_r).reshape(SSM_CHUNK, g * p), jnp.stack(pw_i).reshape(SSM_CHUNK, g * p)])
    return (tile_in(bb_r).astype(BF16), tile_in(bb_i).astype(BF16),
            tile_out(lw['c_re'].astype(F32)).astype(BF16), tile_out(lw['c_im'].astype(F32)).astype(BF16),
            pw, lw['d'].astype(F32).reshape(1, g * h))


def _ssm_scan_kernel(*refs, n_ucols, lane_block):
    u_refs = refs[:n_ucols]
    (bbr_ref, bbi_ref, ccr_ref, cci_ref, pw_ref, d_ref, y_ref, fin_ref, xr, xi, cr, ci, carry, yperm) = refs[n_ucols:]
    tiles, tin, tst = bbr_ref.shape
    tb, sw = xr.shape
    chunks = tb // SSM_CHUNK

    @pl.when(pl.program_id(1) == 0)
    def _():
        carry[...] = jnp.zeros_like(carry)

    u = jnp.concatenate(
        [jnp.concatenate([r[pl.ds(j, chunks, stride=SSM_CHUNK), :] for j in range(SSM_CHUNK)], axis=0)
         for r in u_refs], axis=-1)
    ub = u.astype(BF16)
    for k in range(tiles):
        xr[:, k * tst:(k + 1) * tst] = _dot(ub[:, k * tin:(k + 1) * tin], bbr_ref[k])
        xi[:, k * tst:(k + 1) * tst] = _dot(ub[:, k * tin:(k + 1) * tin], bbi_ref[k])

    step_rows = lambda j: pl.ds(j * chunks, chunks)
    for lb in range(sw // lane_block):
        cols = pl.ds(lb * lane_block, lane_block)
        lr, li = pw_ref[0, 0:1, cols], pw_ref[1, 0:1, cols]
        sr, si = xr[step_rows(0), cols], xi[step_rows(0), cols]
        for j in range(1, SSM_CHUNK):
            nr = lr * sr - li * si + xr[step_rows(j), cols]
            ni = lr * si + li * sr + xi[step_rows(j), cols]
            xr[step_rows(j), cols] = nr
            xi[step_rows(j), cols] = ni
            sr, si = nr, ni

    lcr, lci = pw_ref[0, SSM_CHUNK - 1:SSM_CHUNK, :], pw_ref[1, SSM_CHUNK - 1:SSM_CHUNK, :]

    def chunk_step(c, state):
        kr, ki = state
        cr[pl.ds(c, 1), :] = kr
        ci[pl.ds(c, 1), :] = ki
        last = (SSM_CHUNK - 1) * chunks + c
        return (lcr * kr - lci * ki + xr[pl.ds(last, 1), :], lcr * ki + lci * kr + xi[pl.ds(last, 1), :])

    kr, ki = lax.fori_loop(0, chunks, chunk_step, (carry[0:1, :], carry[1:2, :]))
    carry[0:1, :] = kr
    carry[1:2, :] = ki
    fin_ref[0:1, :] = kr
    fin_ref[1:2, :] = ki

    for lb in range(sw // lane_block):
        cols = pl.ds(lb * lane_block, lane_block)
        ckr, cki = cr[:, cols], ci[:, cols]
        for j in range(SSM_CHUNK):
            pr, pi = pw_ref[0, j:j + 1, cols], pw_ref[1, j:j + 1, cols]
            xr[step_rows(j), cols] = xr[step_rows(j), cols] + (pr * ckr - pi * cki)
            xi[step_rows(j), cols] = xi[step_rows(j), cols] + (pr * cki + pi * ckr)

    per_tile = tin // LANES
    for k in range(tiles):
        st = pl.ds(k * tst, tst)
        y = (_dot(xr[:, st].astype(BF16), ccr_ref[k]) - _dot(xi[:, st].astype(BF16), cci_ref[k])
             + u[:, k * tin:(k + 1) * tin] * d_ref[:, k * tin:(k + 1) * tin])
        for l in range(per_tile):
            yperm[k * per_tile + l] = y[:, l * LANES:(l + 1) * LANES]
    for c in range(chunks):
        for l in range(n_ucols):
            y_ref[c * SSM_CHUNK:(c + 1) * SSM_CHUNK, l * LANES:(l + 1) * LANES] = (
                yperm.at[l][pl.ds(c, SSM_CHUNK, stride=chunks), :])


def ssm_scan(z, n, t, lw, *, tb=512, lane_block=256):
    ops = _ssm_scan_operands(lw)
    bbr = ops[0]
    tiles, tin, tst = bbr.shape
    w, sw = tiles * tin, tiles * tst
    tb = _row_tile(t, tb)
    assert tb % (8 * SSM_CHUNK) == 0 and sw % lane_block == 0 and tin % LANES == 0
    nt = t // tb
    n_ucols = w // LANES
    ucol = lambda l: pl.BlockSpec((tb, LANES), lambda i, k: (i * nt + k, l))
    full = lambda a: pl.BlockSpec(a.shape, lambda i, k: (0,) * a.ndim)
    return pl.pallas_call(
        functools.partial(_ssm_scan_kernel, n_ucols=n_ucols, lane_block=lane_block),
        grid=(n, nt),
        in_specs=[ucol(l) for l in range(n_ucols)] + [full(a) for a in ops],
        out_specs=[pl.BlockSpec((tb, w), lambda i, k: (i * nt + k, 0)),
                   pl.BlockSpec((None, 2, sw), lambda i, k: (i, 0, 0))],
        out_shape=[jax.ShapeDtypeStruct((n * t, w), F32), jax.ShapeDtypeStruct((n, 2, sw), F32)],
        scratch_shapes=[pltpu.VMEM((tb, sw), F32), pltpu.VMEM((tb, sw), F32),
                        pltpu.VMEM((tb // SSM_CHUNK, sw), F32), pltpu.VMEM((tb // SSM_CHUNK, sw), F32),
                        pltpu.VMEM((2, sw), F32), pltpu.VMEM((n_ucols, tb, LANES), F32)],
        compiler_params=_cparams("parallel", "arbitrary"),
        name="ssm_scan",
    )(*([z] * n_ucols), *ops)


def _ssm_step_kernel(u_ref, h0r_ref, h0i_ref, bbr_ref, bbi_ref, ccr_ref, cci_ref, lr_ref, li_ref, d_ref,
                     y_ref, sr_ref, si_ref):
    u = u_ref[...]
    ub = u.astype(BF16)
    lr, li = lr_ref[...], li_ref[...]
    h0r, h0i = h0r_ref[...], h0i_ref[...]
    sr = _dot(ub, bbr_ref[...]) + (lr * h0r - li * h0i)
    si = _dot(ub, bbi_ref[...]) + (lr * h0i + li * h0r)
    sr_ref[...] = sr
    si_ref[...] = si
    y_ref[...] = _dot(sr.astype(BF16), ccr_ref[...]) - _dot(si.astype(BF16), cci_ref[...]) + u * d_ref[...]


def ssm_step(u, h0r, h0i, lw):
    lam_r, lam_i, bb_r, bb_i = _ssm_discretize(lw)
    g, p, h = bb_r.shape
    eye = jnp.eye(g, dtype=F32)
    dense_in = lambda b: (b.transpose(0, 2, 1)[:, :, None, :] * eye[:, None, :, None]).reshape(g * h, g * p)
    dense_out = lambda c: (c.transpose(0, 2, 1)[:, :, None, :] * eye[:, None, :, None]).reshape(g * p, g * h)
    n = u.shape[0]
    args = (u, h0r, h0i, dense_in(bb_r).astype(BF16), dense_in(bb_i).astype(BF16),
            dense_out(lw['c_re'].astype(F32)).astype(BF16), dense_out(lw['c_im'].astype(F32)).astype(BF16),
            lam_r.reshape(1, g * p), lam_i.reshape(1, g * p), lw['d'].astype(F32).reshape(1, g * h))
    full = lambda a: pl.BlockSpec(a.shape, lambda i: (0, 0))
    return pl.pallas_call(
        _ssm_step_kernel,
        grid=(1,),
        in_specs=[full(a) for a in args],
        out_specs=[pl.BlockSpec((n, g * h), lambda i: (0, 0)),
                   pl.BlockSpec((n, g * p), lambda i: (0, 0)),
                   pl.BlockSpec((n, g * p), lambda i: (0, 0))],
        out_shape=[jax.ShapeDtypeStruct((n, g * h), F32),
                   jax.ShapeDtypeStruct((n, g * p), F32),
                   jax.ShapeDtypeStruct((n, g * p), F32)],
        compiler_params=_cparams("arbitrary"),
        name="ssm_step",
    )(*args)


def _band_attn_kernel(*refs, n_pairs, kv_pairs, heads, dilation, with_sink, with_lse):
    it = iter(refs)
    q_refs = [next(it) for _ in range(n_pairs)]
    kv_refs = [[next(it) for _ in range(4)] for _ in range(kv_pairs)]
    sink_ref = next(it) if with_sink else None
    o_refs = [next(it) for _ in range(n_pairs)]
    lse_refs = [next(it) for _ in range(n_pairs)] if with_lse else None
    b = pl.program_id(1)
    kj = lax.broadcasted_iota(jnp.int32, (2 * BAND, BAND), 0)
    qi = lax.broadcasted_iota(jnp.int32, (2 * BAND, BAND), 1)
    bias = jnp.where((kj >= qi) & (kj <= qi + BAND) & ((kj >= BAND) | (b > 0)), 0.0, NEG)
    lane = lax.broadcasted_iota(jnp.int32, (1, LANES), 1)
    lane_half = [(lane < HEAD_DIM).astype(F32), (lane >= HEAD_DIM).astype(F32)]
    row = lax.broadcasted_iota(jnp.int32, (LANES, 1), 0)
    row_half = [row < HEAD_DIM, row >= HEAD_DIM]

    def residue(r):
        rows = pl.ds(r, BAND, stride=dilation) if dilation > 1 else pl.ds(0, BAND)
        base, variants = {}, {}

        def kv_variant(pair, rot, half):
            if pair not in base:
                kp_ref, kc_ref, vp_ref, vc_ref = kv_refs[pair]
                k2 = jnp.concatenate([kp_ref[rows, :], kc_ref[rows, :]], axis=0)
                v2 = jnp.concatenate([vp_ref[rows, :], vc_ref[rows, :]], axis=0)
                base[pair] = (k2, v2.T)
            if (pair, rot, half) not in variants:
                k2, vt = base[pair]
                if rot:
                    k2 = pltpu.roll(k2, HEAD_DIM, 1)
                    vt = jnp.concatenate([vt[HEAD_DIM:], vt[:HEAD_DIM]], axis=0)
                variants[(pair, rot, half)] = (k2.astype(BF16), jnp.where(row_half[half], vt, 0.0).astype(BF16))
            return variants[(pair, rot, half)]

        for j in range(n_pairs):
            q2 = q_refs[j][rows, :] * (HEAD_DIM ** -0.5)
            ot = jnp.zeros((LANES, BAND), F32)
            lset = jnp.zeros((LANES, BAND), F32)
            for half in (0, 1):
                pair, rot = heads[j][half]
                kb, vb = kv_variant(pair, rot, half)
                qm = (q2 * lane_half[half]).astype(BF16)
                st = lax.dot_general(kb, qm, (((1,), (1,)), ((), ())), preferred_element_type=F32) + bias
                m = jnp.max(st, axis=0, keepdims=True)
                e = jnp.exp(st - m)
                lse = m + jnp.log(jnp.sum(e, axis=0, keepdims=True))
                if with_sink:
                    sk = sink_ref[2 * j + half]
                    mx = jnp.maximum(lse, sk)
                    lse = mx + jnp.log(jnp.exp(lse - mx) + jnp.exp(sk - mx))
                pt = (e * jnp.exp(m - lse)).astype(BF16)
                ot = ot + _dot(vb, pt)
                if with_lse:
                    lset = jnp.where(row_half[half], lse, lset)
            o_refs[j][rows, :] = ot.T
            if with_lse:
                lse_refs[j][rows, :] = lset.T

    if dilation == 1:
        residue(0)
    else:
        def body(r, c):
            residue(r)
            return c
        lax.fori_loop(0, dilation, body, 0)


def band_attn(src, *, q_col, k_col, v_col, q_heads, kv_heads, dilation=1, sinks=None, with_lse=False):
    s_, t_, _ = src.shape
    blk = BAND * dilation
    assert t_ % blk == 0 and q_heads % 2 == 0 and kv_heads % 2 == 0
    grp = q_heads // kv_heads
    n_pairs = q_heads // 2
    kv_pairs = kv_heads // 2
    assert q_col % LANES == 0 and k_col % LANES == 0 and v_col % LANES == 0
    heads = []
    for j in range(n_pairs):
        pair_heads = []
        for half in (0, 1):
            kvh = (2 * j + half) // grp
            pair_heads.append((kvh // 2, (kvh % 2) != half))
        heads.append(tuple(pair_heads))
    cur = lambda col: pl.BlockSpec((None, blk, LANES), lambda n, b: (n, b, col // LANES))
    prev = lambda col: pl.BlockSpec((None, blk, LANES), lambda n, b: (n, jnp.maximum(b - 1, 0), col // LANES))
    in_specs = [cur(q_col + j * LANES) for j in range(n_pairs)]
    for kp in range(kv_pairs):
        kc, vc = k_col + kp * LANES, v_col + kp * LANES
        in_specs += [prev(kc), cur(kc), prev(vc), cur(vc)]
    args = [src] * len(in_specs)
    if sinks is not None:
        in_specs.append(pl.BlockSpec(memory_space=pltpu.SMEM))
        args.append(sinks.astype(F32))
    n_out = n_pairs * (2 if with_lse else 1)
    out = pl.pallas_call(
        functools.partial(_band_attn_kernel, n_pairs=n_pairs, kv_pairs=kv_pairs, heads=tuple(heads),
                          dilation=dilation, with_sink=sinks is not None, with_lse=with_lse),
        grid=(s_, t_ // blk),
        in_specs=in_specs,
        out_specs=[pl.BlockSpec((None, blk, LANES), lambda n, b: (n, b, 0))] * n_out,
        out_shape=[jax.ShapeDtypeStruct((s_, t_, LANES), F32)] * n_out,
        compiler_params=_cparams("parallel", "arbitrary"),
        name="band_attn",
    )(*args)
    return (out[:n_pairs], out[n_pairs:]) if with_lse else out


def _decode_attn_kernel(*refs, nb, hkv, grp, dilation, with_sink):
    it = iter(refs)
    q_ref, kv_ref, new_ref = next(it), next(it), next(it)
    sink_ref = next(it) if with_sink else None
    o_ref, lse_ref = next(it), next(it)
    rows = kv_ref.shape[-1]
    pos = lax.broadcasted_iota(jnp.int32, (1, rows), 1)
    valid = (pos & (dilation - 1)) == 0
    lane = lax.broadcasted_iota(jnp.int32, (1, LANES), 1)
    scale = HEAD_DIM ** -0.5

    def per_seq(i, c):
        o_acc = jnp.zeros((HEAD_DIM, LANES), F32)
        lse_acc = jnp.zeros((1, LANES), F32)
        for h in range(hkv):
            kt, vt = kv_ref[i, 0, h], kv_ref[i, 1, h]
            kn, vn = new_ref[i, 0, h], new_ref[i, 1, h]
            qh = q_ref[i, h]
            for g in range(grp):
                qc = qh[:, g:g + 1] * scale
                s = jnp.where(valid, jnp.sum(kt * qc, axis=0, keepdims=True), NEG)
                sn = jnp.sum(kn * qc, axis=0, keepdims=True)
                m = jnp.maximum(jnp.max(s, axis=-1, keepdims=True), sn)
                e, en = jnp.exp(s - m), jnp.exp(sn - m)
                lse = m + jnp.log(jnp.sum(e, axis=-1, keepdims=True) + en)
                if with_sink:
                    sk = sink_ref[h * grp + g]
                    mx = jnp.maximum(lse, sk)
                    lse = mx + jnp.log(jnp.exp(lse - mx) + jnp.exp(sk - mx))
                w = jnp.exp(m - lse)
                o = jnp.sum(vt * (e * w), axis=-1, keepdims=True) + vn * (en * w)
                onehot = (lane == h * grp + g).astype(F32)
                o_acc = o_acc + o * onehot
                lse_acc = lse_acc + lse * onehot
        o_ref[i] = o_acc
        lse_ref[i] = lse_acc
        return c

    lax.fori_loop(0, nb, per_seq, 0)


def decode_attn(q, cache, layer, new_kv, *, dilation, sinks=None, block_bytes=4 * 1024 * 1024):
    n, hkv, dh, grp = q.shape
    rows = cache.shape[2]
    assert dilation & (dilation - 1) == 0 and hkv * grp <= LANES
    nb = max(1, min(n, block_bytes // (2 * hkv * dh * rows * 4)))
    while n % nb:
        nb -= 1
    cache_t = cache.transpose(0, 1, 3, 4, 5, 2)
    in_specs = [pl.BlockSpec((nb, hkv, dh, grp), lambda i: (i, 0, 0, 0)),
                pl.BlockSpec((None, nb, 2, hkv, dh, rows), lambda i: (layer, i, 0, 0, 0, 0)),
                pl.BlockSpec((nb, 2, hkv, dh, 1), lambda i: (i, 0, 0, 0, 0))]
    args = [q, cache_t, new_kv]
    if sinks is not None:
        in_specs.append(pl.BlockSpec(memory_space=pltpu.SMEM))
        args.append(sinks.astype(F32))
    o, lse = pl.pallas_call(
        functools.partial(_decode_attn_kernel, nb=nb, hkv=hkv, grp=grp, dilation=dilation,
                          with_sink=sinks is not None),
        grid=(n // nb,),
        in_specs=in_specs,
        out_specs=[pl.BlockSpec((nb, dh, LANES), lambda i: (i, 0, 0)),
                   pl.BlockSpec((nb, 1, LANES), lambda i: (i, 0, 0))],
        out_shape=[jax.ShapeDtypeStruct((n, dh, LANES), F32), jax.ShapeDtypeStruct((n, 1, LANES), F32)],
        compiler_params=_cparams("parallel"),
        name="decode_attn",
    )(*args)
    nh = hkv * grp
    return o[:, :, :nh].transpose(0, 2, 1), lse[:, 0, :nh]


def _merge_kernel(*refs, nb_pairs, nc_pairs, n_groups):
    it = iter(refs)
    x_ref, ya_ref = next(it), next(it)
    lanes = lambda k: jnp.concatenate([next(it)[...] for _ in range(k)], axis=-1)
    ob = lanes(nb_pairs)
    ocs = [lanes(nc_pairs) for _ in range(n_groups)]
    lses = [lanes(nc_pairs) for _ in range(n_groups)]
    gates = [jax.nn.sigmoid(lanes(2)) for _ in range(3)]
    wglu_ref, bglu_ref, wba_ref, wbb_ref, wbc_ref, wout_ref, o_ref = (next(it) for _ in range(7))
    g = jax.nn.gelu(ya_ref[...])
    oa = g * jax.nn.sigmoid(_dot(g.astype(BF16), wglu_ref[...]) + bglu_ref[...])
    mx = functools.reduce(jnp.maximum, lses)
    es = [jnp.exp(l - mx) for l in lses]
    oc = sum(e * o for e, o in zip(es, ocs)) / sum(es)
    merged = (gates[0] * _dot(oa.astype(BF16), wba_ref[...])
              + gates[1] * _dot(ob.astype(BF16), wbb_ref[...])
              + gates[2] * _dot(oc.astype(BF16), wbc_ref[...]))
    o_ref[...] = x_ref[...] + _dot(merged.astype(BF16), wout_ref[...])


def merge(x, z, gate_col, ya, ob, ocs, lses, lw, *, tm=512):
    rows, d = x.shape
    tm = _row_tile(rows, tm)
    half = d // 2
    assert gate_col % half == 0
    row = lambda w: pl.BlockSpec((tm, w), lambda i: (i, 0))
    gate = lambda k: pl.BlockSpec((tm, half), lambda i: (i, gate_col // half + k))
    full = lambda a: pl.BlockSpec(a.shape, lambda i: (0, 0))
    weights = (lw['w_glu'].astype(BF16), lw['b_glu'].astype(F32).reshape(1, -1), lw['w_branch_a'].astype(BF16),
               lw['w_branch_b'].astype(BF16), lw['w_branch_c'].astype(BF16), lw['w_out'].astype(BF16))
    pairs = list(ob) + [a for grp in ocs for a in grp] + [a for grp in lses for a in grp]
    return pl.pallas_call(
        functools.partial(_merge_kernel, nb_pairs=len(ob), nc_pairs=len(ocs[0]), n_groups=len(ocs)),
        grid=(rows // tm,),
        in_specs=([row(d), row(ya.shape[1])] + [row(LANES)] * len(pairs) + [gate(k) for k in range(6)]
                  + [full(w) for w in weights]),
        out_specs=row(d),
        out_shape=jax.ShapeDtypeStruct((rows, d), F32),
        compiler_params=_cparams("parallel"),
        name="merge",
    )(x, ya, *pairs, *([z] * 6), *weights)


def _mlp_kernel(x_ref, g_ref, wup_ref, wdn_ref, o_ref, hn_ref, acc_ref):
    k = pl.program_id(1)

    @pl.when(k == 0)
    def _():
        hn_ref[...] = _rmsnorm_bf16(x_ref[...], g_ref[...])
        acc_ref[...] = jnp.zeros_like(acc_ref)

    h = jnp.maximum(_dot(hn_ref[...], wup_ref[...]), 0.0)
    acc_ref[...] += _dot((h * h).astype(BF16), wdn_ref[...])

    @pl.when(k == pl.num_programs(1) - 1)
    def _():
        o_ref[...] = x_ref[...] + acc_ref[...]


def mlp(x, g, w_up, w_down, *, tm=1024, tk=512):
    rows, d = x.shape
    dff = w_up.shape[1]
    tm = _row_tile(rows, tm)
    assert dff % tk == 0
    return pl.pallas_call(
        _mlp_kernel,
        grid=(rows // tm, dff // tk),
        in_specs=[pl.BlockSpec((tm, d), lambda i, k: (i, 0)),
                  pl.BlockSpec((1, d), lambda i, k: (0, 0)),
                  pl.BlockSpec((d, tk), lambda i, k: (0, k)),
                  pl.BlockSpec((tk, d), lambda i, k: (k, 0))],
        out_specs=pl.BlockSpec((tm, d), lambda i, k: (i, 0)),
        out_shape=jax.ShapeDtypeStruct((rows, d), F32),
        scratch_shapes=[pltpu.VMEM((tm, d), BF16), pltpu.VMEM((tm, d), F32)],
        compiler_params=_cparams("parallel", "arbitrary"),
        name="mlp",
    )(x, g, w_up, w_down)


def _ple_kernel(x_ref, p_ref, g_ref, wg_ref, wp_ref, gf_ref, o_ref, *, final_norm):
    x = x_ref[...]
    gate = jax.nn.sigmoid(_dot(_rmsnorm_bf16(x, g_ref[...]), wg_ref[...]))
    y = x + gate * _dot(p_ref[...].astype(BF16), wp_ref[...])
    if final_norm:
        ms = jnp.mean(y * y, axis=-1, keepdims=True)
        y = y * lax.rsqrt(ms + EPS) * gf_ref[...]
    o_ref[...] = y


def ple(x, p, layer, g, w_gate, w_proj, g_final, *, final_norm, tm=512):
    rows, d = x.shape
    tm = _row_tile(rows, tm)
    row = lambda w: pl.BlockSpec((tm, w), lambda i: (i, 0))
    full = lambda a: pl.BlockSpec(a.shape, lambda i: (0, 0))
    return pl.pallas_call(
        functools.partial(_ple_kernel, final_norm=final_norm),
        grid=(rows // tm,),
        in_specs=[row(d), pl.BlockSpec((None, tm, p.shape[2]), lambda i: (layer, i, 0)),
                  full(g), full(w_gate), full(w_proj), full(g_final)],
        out_specs=row(d),
        out_shape=jax.ShapeDtypeStruct((rows, d), F32),
        compiler_params=_cparams("parallel"),
        name="ple",
    )(x, p, g, w_gate, w_proj, g_final)


def _column_offsets():
    ssm_w = 32 * SSM_GROUP
    swa_q, swa_kv, dil_w = SWA_Q_HEADS * HEAD_DIM, SWA_KV_HEADS * HEAD_DIM, DIL_HEADS * HEAD_DIM
    off = {'u': 0, 'swa_q': ssm_w, 'swa_k': ssm_w + swa_q, 'swa_v': ssm_w + swa_q + swa_kv}
    off['dil'] = ssm_w + swa_q + 2 * swa_kv
    off['gates'] = off['dil'] + 3 * len(DIL_PAIRS) * dil_w
    return off


def _prompt_mixer(z, n, t, lw):
    off = _column_offsets()
    rows = n * t
    z3 = z.reshape(n, t, -1)
    dil_w = DIL_HEADS * HEAD_DIM

    ya, fin = ssm_scan(z, n, t, lw)
    groups = fin.shape[-1] // SSM_STATE
    ssm_new = jnp.stack([fin[:, 0].reshape(n, groups, SSM_STATE), fin[:, 1].reshape(n, groups, SSM_STATE)], axis=-1)

    flat = lambda pairs: [a.reshape(rows, LANES) for a in pairs]
    ob = flat(band_attn(z3, q_col=off['swa_q'], k_col=off['swa_k'], v_col=off['swa_v'],
                        q_heads=SWA_Q_HEADS, kv_heads=SWA_KV_HEADS, sinks=lw['sinks']))
    keep = min(BAND, t)
    swa_new = z3[:, t - keep:, off['swa_k']:off['swa_k'] + 2 * SWA_KV_HEADS * HEAD_DIM].reshape(
        n, keep, 2, SWA_KV_HEADS, HEAD_DIM)

    ocs, lses, dil_new = [], [], []
    for gi, (win, dil) in enumerate(DIL_PAIRS):
        assert win == BAND * dil
        c0 = off['dil'] + 3 * dil_w * gi
        o, lse = band_attn(z3, q_col=c0, k_col=c0 + dil_w, v_col=c0 + 2 * dil_w,
                           q_heads=DIL_HEADS, kv_heads=DIL_HEADS, dilation=dil, with_lse=True)
        ocs.append(flat(o))
        lses.append(flat(lse))
        keep = min(win, t)
        dil_new.append(z3[:, t - keep:, c0 + dil_w:c0 + 3 * dil_w].reshape(n, keep, 2, DIL_HEADS, HEAD_DIM))
    return ya, ob, ocs, lses, (swa_new, dil_new[0], dil_new[1], dil_new[2], ssm_new)


def _sample_mixer(z, lw, cache):
    off = _column_offsets()
    n = z.shape[0]
    dil_w = DIL_HEADS * HEAD_DIM
    ssm_w = off['swa_q']
    layer = cache['layer']

    h0 = cache['ssm'].astype(F32)
    ya, s_r, s_i = ssm_step(z[:, :ssm_w], h0[..., 0].reshape(n, -1), h0[..., 1].reshape(n, -1), lw)
    ssm_new = jnp.stack([s_r.reshape(h0.shape[:-1]), s_i.reshape(h0.shape[:-1])], axis=-1)

    def attend(q_col, kv_col, q_heads, kv_heads, buf, dilation, sinks=None):
        grp = q_heads // kv_heads
        q = z[:, q_col:q_col + q_heads * HEAD_DIM].reshape(n, kv_heads, grp, HEAD_DIM).swapaxes(2, 3)
        new_kv = z[:, kv_col:kv_col + 2 * kv_heads * HEAD_DIM].reshape(n, 2, kv_heads, HEAD_DIM)
        o, lse = decode_attn(q, buf, layer, new_kv[..., None], dilation=dilation, sinks=sinks)
        pairs = lambda a: [a[:, j * LANES:(j + 1) * LANES] for j in range(q_heads * HEAD_DIM // LANES)]
        return (pairs(o.reshape(n, q_heads * HEAD_DIM)), pairs(jnp.repeat(lse, HEAD_DIM, axis=1)),
                new_kv[:, None])

    ob, _, swa_new = attend(off['swa_q'], off['swa_k'], SWA_Q_HEADS, SWA_KV_HEADS, cache['swa'], 1,
                            sinks=lw['sinks'])
    ocs, lses, dil_new = [], [], []
    for gi, (win, dil) in enumerate(DIL_PAIRS):
        c0 = off['dil'] + 3 * dil_w * gi
        o, lse, new = attend(c0, c0 + dil_w, DIL_HEADS, DIL_HEADS, cache['dil'][gi], dil)
        ocs.append(o)
        lses.append(lse)
        dil_new.append(new)
    return ya, ob, ocs, lses, (swa_new, dil_new[0], dil_new[1], dil_new[2], ssm_new)


def _layer(x, p, layer, lw, cache, n, t, g_final, final_norm):
    off = _column_offsets()
    row1 = lambda a: a.astype(F32).reshape(1, -1)
    z = norm_matmul(x, row1(lw['g_mix']), lw['w_in'].astype(BF16))
    if cache is None:
        ya, ob, ocs, lses, st = _prompt_mixer(z, n, t, lw)
    else:
        assert t == 1
        win_rows = [c.shape[2] for c in (cache['swa'],) + tuple(cache['dil'])]
        assert win_rows == [BAND] + [w for w, _ in DIL_PAIRS], win_rows
        ya, ob, ocs, lses, st = _sample_mixer(z, lw, cache)
    x = merge(x, z, off['gates'], ya, ob, ocs, lses, lw)
    x = mlp(x, row1(lw['g_mlp']), lw['w_up'].astype(BF16), lw['w_down'].astype(BF16))
    x = ple(x, p, layer, row1(lw['g_ple']), lw['w_ple_gate'].astype(BF16), lw['w_ple_proj'].astype(BF16),
            row1(g_final), final_norm=final_norm)
    return x, st


def kernel(x_prompt, x_sample, cache_swa_kv, cache_dil_d1_kv, cache_dil_d4_kv, cache_dil_d16_kv, state_ssm, p_prompt, p_sample, w_in, g_mix, ssm_a_re, ssm_a_im, ssm_log_dt, ssm_b_re, ssm_b_im, ssm_c_re, ssm_c_im, ssm_d, w_glu, b_glu, attn_sinks, w_branch_a, w_branch_b, w_branch_c, w_out, g_mlp, w_up, w_down, g_ple, w_ple_gate, w_ple_proj, g_final):
    depth = w_in.shape[0]
    bp, tp, d = x_prompt.shape
    bs, ts, _ = x_sample.shape
    yp = x_prompt.reshape(bp * tp, d)
    ys = x_sample.reshape(bs * ts, d)
    st_p, st_s = [], []
    for l in range(depth):
        lw = {'w_in': w_in[l], 'g_mix': g_mix[l], 'a_re': ssm_a_re[l], 'a_im': ssm_a_im[l],
              'log_dt': ssm_log_dt[l], 'b_re': ssm_b_re[l], 'b_im': ssm_b_im[l], 'c_re': ssm_c_re[l],
              'c_im': ssm_c_im[l], 'd': ssm_d[l], 'w_glu': w_glu[l], 'b_glu': b_glu[l], 'sinks': attn_sinks[l],
              'w_branch_a': w_branch_a[l], 'w_branch_b': w_branch_b[l], 'w_branch_c': w_branch_c[l],
              'w_out': w_out[l], 'g_mlp': g_mlp[l], 'w_up': w_up[l], 'w_down': w_down[l], 'g_ple': g_ple[l],
              'w_ple_gate': w_ple_gate[l], 'w_ple_proj': w_ple_proj[l]}
        cache_l = {'layer': l, 'swa': cache_swa_kv,
                   'dil': (cache_dil_d1_kv, cache_dil_d4_kv, cache_dil_d16_kv),
                   'ssm': state_ssm[l]}
        last = l == depth - 1
        yp, sp = _layer(yp, p_prompt.reshape(depth, bp * tp, -1), l, lw, None, bp, tp, g_final, last)
        ys, ss = _layer(ys, p_sample.reshape(depth, bs * ts, -1), l, lw, cache_l, bs, ts, g_final, last)
        st_p.append(sp)
        st_s.append(ss)
    stk = lambda sts, i: jnp.stack([s[i] for s in sts])
    return (yp.reshape(bp, tp, d), ys.reshape(bs, ts, d),
            stk(st_p, 0), stk(st_p, 1), stk(st_p, 2), stk(st_p, 3), stk(st_p, 4),
            stk(st_s, 0), stk(st_s, 1), stk(st_s, 2), stk(st_s, 3), stk(st_s, 4))
```

```python
import functools

import jax
import jax.numpy as jnp
from jax import lax
from jax.experimental import pallas as pl
from jax.experimental.pallas import tpu as pltpu

HEAD_DIM = 64
SSM_GROUP = 16
SSM_STATE = 64
SWA_Q_HEADS = 8
SWA_KV_HEADS = 2
DIL_HEADS = 4
DIL_PAIRS = ((128, 1), (512, 4), (2048, 16))
BAND = 128
SSM_CHUNK = 16
SSM_TILE_GROUPS = 16
EPS = 1e-6
LANES = 128
NEG = -1e30
VMEM_LIMIT = 48 * 1024 * 1024
BF16 = jnp.bfloat16
F32 = jnp.float32


def _cparams(*sem):
    return pltpu.CompilerParams(dimension_semantics=sem, vmem_limit_bytes=VMEM_LIMIT)


def _rmsnorm_bf16(xf, g):
    ms = jnp.mean(xf * xf, axis=-1, keepdims=True)
    return (xf * lax.rsqrt(ms + EPS) * g).astype(BF16)


def _dot(a, b):
    return jnp.dot(a, b, preferred_element_type=F32)


def _row_tile(rows, pref):
    t = min(rows, pref)
    assert rows % t == 0, (rows, pref)
    return t


def _norm_matmul_kernel(x_ref, g_ref, w_ref, o_ref, hn_ref):
    @pl.when(pl.program_id(1) == 0)
    def _():
        hn_ref[...] = _rmsnorm_bf16(x_ref[...], g_ref[...])

    o_ref[...] = _dot(hn_ref[...], w_ref[...])


def norm_matmul(x, g, w, *, tm=2048, tn=512):
    rows, d = x.shape
    cols = w.shape[1]
    tm = _row_tile(rows, tm)
    assert cols % tn == 0
    return pl.pallas_call(
        _norm_matmul_kernel,
        grid=(rows // tm, cols // tn),
        in_specs=[pl.BlockSpec((tm, d), lambda i, j: (i, 0)),
                  pl.BlockSpec((1, d), lambda i, j: (0, 0)),
                  pl.BlockSpec((d, tn), lambda i, j: (0, j))],
        out_specs=pl.BlockSpec((tm, tn), lambda i, j: (i, j)),
        out_shape=jax.ShapeDtypeStruct((rows, cols), F32),
        scratch_shapes=[pltpu.VMEM((tm, d), BF16)],
        compiler_params=_cparams("parallel", "arbitrary"),
        name="norm_matmul",
    )(x, g, w)


def _ssm_discretize(lw):
    a_re, a_im = lw['a_re'].astype(F32), lw['a_im'].astype(F32)
    dt = jnp.exp(lw['log_dt'].astype(F32))[:, None]
    mag = jnp.exp(a_re * dt)
    lam_r, lam_i = mag * jnp.cos(a_im * dt), mag * jnp.sin(a_im * dt)
    den = a_re * a_re + a_im * a_im
    zr = ((lam_r - 1.0) * a_re + lam_i * a_im) / den
    zi = (lam_i * a_re - (lam_r - 1.0) * a_im) / den
    b_re, b_im = lw['b_re'].astype(F32), lw['b_im'].astype(F32)
    bb_r = zr[..., None] * b_re - zi[..., None] * b_im
    bb_i = zr[..., None] * b_im + zi[..., None] * b_re
    return lam_r, lam_i, bb_r, bb_i


def _ssm_scan_operands(lw):
    lam_r, lam_i, bb_r, bb_i = _ssm_discretize(lw)
    g, p, h = bb_r.shape
    tg = SSM_TILE_GROUPS
    eye = jnp.eye(tg, dtype=F32)
    tile_in = lambda b: (b.reshape(g // tg, tg, p, h).transpose(0, 1, 3, 2)[:, :, :, None, :]
                         * eye[None, :, None, :, None]).reshape(g // tg, tg * h, tg * p)
    tile_out = lambda c: (c.reshape(g // tg, tg, h, p).transpose(0, 1, 3, 2)[:, :, :, None, :]
                          * eye[None, :, None, :, None]).reshape(g // tg, tg * p, tg * h)
    pw_r, pw_i = [lam_r], [lam_i]
    for _ in range(SSM_CHUNK - 1):
        pr, pi = pw_r[-1], pw_i[-1]
        pw_r.append(pr * lam_r - pi * lam_i)
        pw_i.append(pr * lam_i + pi * lam_r)
    pw = jnp.stack([jnp.stack(pw_r).reshape(SSM_CHUNK, g * p), jnp.stack(pw_i).reshape(SSM_CHUNK, g * p)])
    return (tile_in(bb_r).astype(BF16), tile_in(bb_i).astype(BF16),
            tile_out(lw['c_re'].astype(F32)).astype(BF16), tile_out(lw['c_im'].astype(F32)).astype(BF16),
            pw, lw['d'].astype(F32).reshape(1, g * h))


def _ssm_scan_kernel(*refs, n_ucols, lane_block):
    u_refs = refs[:n_ucols]
    (bbr_ref, bbi_ref, ccr_ref, cci_ref, pw_ref, d_ref, y_ref, fin_ref, xr, xi, cr, ci, carry, yperm) = refs[n_ucols:]
    tiles, tin, tst = bbr_ref.shape
    tb, sw = xr.shape
    chunks = tb // SSM_CHUNK

    @pl.when(pl.program_id(1) == 0)
    def _():
        carry[...] = jnp.zeros_like(carry)

    u = jnp.concatenate(
        [jnp.concatenate([r[pl.ds(j, chunks, stride=SSM_CHUNK), :] for j in range(SSM_CHUNK)], axis=0)
         for r in u_refs], axis=-1)
    ub = u.astype(BF16)
    for k in range(tiles):
        xr[:, k * tst:(k + 1) * tst] = _dot(ub[:, k * tin:(k + 1) * tin], bbr_ref[k])
        xi[:, k * tst:(k + 1) * tst] = _dot(ub[:, k * tin:(k + 1) * tin], bbi_ref[k])

    step_rows = lambda j: pl.ds(j * chunks, chunks)
    for lb in range(sw // lane_block):
        cols = pl.ds(lb * lane_block, lane_block)
        lr, li = pw_ref[0, 0:1, cols], pw_ref[1, 0:1, cols]
        sr, si = xr[step_rows(0), cols], xi[step_rows(0), cols]
        for j in range(1, SSM_CHUNK):
            nr = lr * sr - li * si + xr[step_rows(j), cols]
            ni = lr * si + li * sr + xi[step_rows(j), cols]
            xr[step_rows(j), cols] = nr
            xi[step_rows(j), cols] = ni
            sr, si = nr, ni

    lcr, lci = pw_ref[0, SSM_CHUNK - 1:SSM_CHUNK, :], pw_ref[1, SSM_CHUNK - 1:SSM_CHUNK, :]

    def chunk_step(c, state):
        kr, ki = state
        cr[pl.ds(c, 1), :] = kr
        ci[pl.ds(c, 1), :] = ki
        last = (SSM_CHUNK - 1) * chunks + c
        return (lcr * kr - lci * ki + xr[pl.ds(last, 1), :], lcr * ki + lci * kr + xi[pl.ds(last, 1), :])

    kr, ki = lax.fori_loop(0, chunks, chunk_step, (carry[0:1, :], carry[1:2, :]))
    carry[0:1, :] = kr
    carry[1:2, :] = ki
    fin_ref[0:1, :] = kr
    fin_ref[1:2, :] = ki

    for lb in range(sw // lane_block):
        cols = pl.ds(lb * lane_block, lane_block)
        ckr, cki = cr[:, cols], ci[:, cols]
        for j in range(SSM_CHUNK):
            pr, pi = pw_ref[0, j:j + 1, cols], pw_ref[1, j:j + 1, cols]
            xr[step_rows(j), cols] = xr[step_rows(j), cols] + (pr * ckr - pi * cki)
            xi[step_rows(j), cols] = xi[step_rows(j), cols] + (pr * cki + pi * ckr)

    per_tile = tin // LANES
    for k in range(tiles):
        st = pl.ds(k * tst, tst)
        y = (_dot(xr[:, st].astype(BF16), ccr_ref[k]) - _dot(xi[:, st].astype(BF16), cci_ref[k])
             + u[:, k * tin:(k + 1) * tin] * d_ref[:, k * tin:(k + 1) * tin])
        for l in range(per_tile):
            yperm[k * per_tile + l] = y[:, l * LANES:(l + 1) * LANES]
    for c in range(chunks):
        for l in range(n_ucols):
            y_ref[c * SSM_CHUNK:(c + 1) * SSM_CHUNK, l * LANES:(l + 1) * LANES] = (
                yperm.at[l][pl.ds(c, SSM_CHUNK, stride=chunks), :])


def ssm_scan(z, n, t, lw, *, tb=512, lane_block=256):
    ops = _ssm_scan_operands(lw)
    bbr = ops[0]
    tiles, tin, tst = bbr.shape
    w, sw = tiles * tin, tiles * tst
    tb = _row_tile(t, tb)
    assert tb % (8 * SSM_CHUNK) == 0 and sw % lane_block == 0 and tin % LANES == 0
    nt = t // tb
    n_ucols = w // LANES
    ucol = lambda l: pl.BlockSpec((tb, LANES), lambda i, k: (i * nt + k, l))
    full = lambda a: pl.BlockSpec(a.shape, lambda i, k: (0,) * a.ndim)
    return pl.pallas_call(
        functools.partial(_ssm_scan_kernel, n_ucols=n_ucols, lane_block=lane_block),
        grid=(n, nt),
        in_specs=[ucol(l) for l in range(n_ucols)] + [full(a) for a in ops],
        out_specs=[pl.BlockSpec((tb, w), lambda i, k: (i * nt + k, 0)),
                   pl.BlockSpec((None, 2, sw), lambda i, k: (i, 0, 0))],
        out_shape=[jax.ShapeDtypeStruct((n * t, w), F32), jax.ShapeDtypeStruct((n, 2, sw), F32)],
        scratch_shapes=[pltpu.VMEM((tb, sw), F32), pltpu.VMEM((tb, sw), F32),
                        pltpu.VMEM((tb // SSM_CHUNK, sw), F32), pltpu.VMEM((tb // SSM_CHUNK, sw), F32),
                        pltpu.VMEM((2, sw), F32), pltpu.VMEM((n_ucols, tb, LANES), F32)],
        compiler_params=_cparams("parallel", "arbitrary"),
        name="ssm_scan",
    )(*([z] * n_ucols), *ops)


def _ssm_step_kernel(u_ref, h0r_ref, h0i_ref, bbr_ref, bbi_ref, ccr_ref, cci_ref, lr_ref, li_ref, d_ref,
                     y_ref, sr_ref, si_ref):
    u = u_ref[...]
    ub = u.astype(BF16)
    lr, li = lr_ref[...], li_ref[...]
    h0r, h0i = h0r_ref[...], h0i_ref[...]
    sr = _dot(ub, bbr_ref[...]) + (lr * h0r - li * h0i)
    si = _dot(ub, bbi_ref[...]) + (lr * h0i + li * h0r)
    sr_ref[...] = sr
    si_ref[...] = si
    y_ref[...] = _dot(sr.astype(BF16), ccr_ref[...]) - _dot(si.astype(BF16), cci_ref[...]) + u * d_ref[...]


def ssm_step(u, h0r, h0i, lw):
    lam_r, lam_i, bb_r, bb_i = _ssm_discretize(lw)
    g, p, h = bb_r.shape
    eye = jnp.eye(g, dtype=F32)
    dense_in = lambda b: (b.transpose(0, 2, 1)[:, :, None, :] * eye[:, None, :, None]).reshape(g * h, g * p)
    dense_out = lambda c: (c.transpose(0, 2, 1)[:, :, None, :] * eye[:, None, :, None]).reshape(g * p, g * h)
    n = u.shape[0]
    args = (u, h0r, h0i, dense_in(bb_r).astype(BF16), dense_in(bb_i).astype(BF16),
            dense_out(lw['c_re'].astype(F32)).astype(BF16), dense_out(lw['c_im'].astype(F32)).astype(BF16),
            lam_r.reshape(1, g * p), lam_i.reshape(1, g * p), lw['d'].astype(F32).reshape(1, g * h))
    full = lambda a: pl.BlockSpec(a.shape, lambda i: (0, 0))
    return pl.pallas_call(
        _ssm_step_kernel,
        grid=(1,),
        in_specs=[full(a) for a in args],
        out_specs=[pl.BlockSpec((n, g * h), lambda i: (0, 0)),
                   pl.BlockSpec((n, g * p), lambda i: (0, 0)),
                   pl.BlockSpec((n, g * p), lambda i: (0, 0))],
        out_shape=[jax.ShapeDtypeStruct((n, g * h), F32),
                   jax.ShapeDtypeStruct((n, g * p), F32),
                   jax.ShapeDtypeStruct((n, g * p), F32)],
        compiler_params=_cparams("arbitrary"),
        name="ssm_step",
    )(*args)


def _band_attn_kernel(*refs, n_pairs, kv_pairs, heads, dilation, nblk, with_sink, with_lse):
    it = iter(refs)
    q_refs = [next(it) for _ in range(n_pairs)]
    kv_refs = [[next(it) for _ in range(4)] for _ in range(kv_pairs)]
    sink_ref = next(it) if with_sink else None
    o_refs = [next(it) for _ in range(n_pairs)]
    lse_refs = [next(it) for _ in range(n_pairs)] if with_lse else None
    b = pl.program_id(1)
    kj = lax.broadcasted_iota(jnp.int32, (2 * BAND, BAND), 0)
    qi = lax.broadcasted_iota(jnp.int32, (2 * BAND, BAND), 1)
    in_band = (kj >= qi) & (kj <= qi + BAND)
    bias_rest = jnp.where(in_band, 0.0, NEG)
    bias_first = jnp.where(in_band & ((kj >= BAND) | (b > 0)), 0.0, NEG)
    lane = lax.broadcasted_iota(jnp.int32, (1, LANES), 1)
    lane_half = [(lane < HEAD_DIM).astype(F32), (lane >= HEAD_DIM).astype(F32)]
    row = lax.broadcasted_iota(jnp.int32, (LANES, 1), 0)
    row_half = [row < HEAD_DIM, row >= HEAD_DIM]

    members = {}
    for j in range(n_pairs):
        for half in (0, 1):
            members.setdefault(heads[j][half], []).append((half, j))

    def block(r, i):
        rows = pl.ds(r, BAND, stride=dilation) if dilation > 1 else pl.ds(i * BAND, BAND)
        bias = bias_first if i == 0 else bias_rest

        def with_prev(prev_ref, cur_ref):
            if i > 0:
                prev = cur_ref[pl.ds((i - 1) * BAND, BAND), :]
            else:
                prev = prev_ref[rows, :] if dilation > 1 else prev_ref[...]
            return jnp.concatenate([prev, cur_ref[rows, :]], axis=0)

        q2 = [q_refs[j][rows, :] * (HEAD_DIM ** -0.5) for j in range(n_pairs)]
        ot = [jnp.zeros((LANES, BAND), F32) for _ in range(n_pairs)]
        lset = [jnp.zeros((LANES, BAND), F32) for _ in range(n_pairs)]
        base = {}
        for (pair, rot), group in members.items():
            group = sorted(group)
            if pair not in base:
                kp_ref, kc_ref, vp_ref, vc_ref = kv_refs[pair]
                base[pair] = (with_prev(kp_ref, kc_ref), with_prev(vp_ref, vc_ref).T)
            k2, vt = base[pair]
            if rot:
                k2 = pltpu.roll(k2, HEAD_DIM, 1)
                vt = jnp.concatenate([vt[HEAD_DIM:], vt[:HEAD_DIM]], axis=0)
            nh = len(group)
            qm = jnp.concatenate([(q2[j] * lane_half[half]).astype(BF16) for half, j in group], axis=0)
            st = lax.dot_general(k2.astype(BF16), qm, (((1,), (1,)), ((), ())), preferred_element_type=F32)
            st = st + jnp.concatenate([bias] * nh, axis=1)
            m = jnp.max(st, axis=0, keepdims=True)
            e = jnp.exp(st - m)
            lse = m + jnp.log(jnp.sum(e, axis=0, keepdims=True))
            if with_sink:
                sk = jnp.concatenate([jnp.full((1, BAND), sink_ref[2 * j + half], F32) for half, j in group], axis=1)
                mx = jnp.maximum(lse, sk)
                lse = mx + jnp.log(jnp.exp(lse - mx) + jnp.exp(sk - mx))
            pt = (e * jnp.exp(m - lse)).astype(BF16)
            for half in (0, 1):
                idx = [g for g, (hf, _) in enumerate(group) if hf == half]
                if not idx:
                    continue
                vb = jnp.where(row_half[half], vt, 0.0).astype(BF16)
                res = _dot(vb, pt[:, idx[0] * BAND:(idx[-1] + 1) * BAND])
                for n_, g in enumerate(idx):
                    j = group[g][1]
                    ot[j] = ot[j] + res[:, n_ * BAND:(n_ + 1) * BAND]
                    if with_lse:
                        lset[j] = jnp.where(row_half[half], lse[:, g * BAND:(g + 1) * BAND], lset[j])
        for j in range(n_pairs):
            o_refs[j][rows, :] = ot[j].T
            if with_lse:
                lse_refs[j][rows, :] = lset[j].T

    if dilation == 1:
        for i in range(nblk):
            block(0, i)
    else:
        def body(r, c):
            block(r, 0)
            return c
        lax.fori_loop(0, dilation, body, 0, unroll=min(dilation, nblk))


def band_attn(src, *, q_col, k_col, v_col, q_heads, kv_heads, dilation=1, sinks=None, with_lse=False, nblk=4):
    s_, t_, _ = src.shape
    blk = BAND * (dilation if dilation > 1 else nblk)
    assert t_ % blk == 0 and q_heads % 2 == 0 and kv_heads % 2 == 0
    grp = q_heads // kv_heads
    n_pairs = q_heads // 2
    kv_pairs = kv_heads // 2
    assert q_col % LANES == 0 and k_col % LANES == 0 and v_col % LANES == 0
    heads = []
    for j in range(n_pairs):
        pair_heads = []
        for half in (0, 1):
            kvh = (2 * j + half) // grp
            pair_heads.append((kvh // 2, (kvh % 2) != half))
        heads.append(tuple(pair_heads))
    cur = lambda col: pl.BlockSpec((None, blk, LANES), lambda n, b: (n, b, col // LANES))
    if dilation > 1:
        prev = lambda col: pl.BlockSpec((None, blk, LANES), lambda n, b: (n, jnp.maximum(b - 1, 0), col // LANES))
    else:
        prev = lambda col: pl.BlockSpec((None, BAND, LANES),
                                        lambda n, b: (n, jnp.maximum(b * nblk - 1, 0), col // LANES))
    in_specs = [cur(q_col + j * LANES) for j in range(n_pairs)]
    for kp in range(kv_pairs):
        kc, vc = k_col + kp * LANES, v_col + kp * LANES
        in_specs += [prev(kc), cur(kc), prev(vc), cur(vc)]
    args = [src] * len(in_specs)
    if sinks is not None:
        in_specs.append(pl.BlockSpec(memory_space=pltpu.SMEM))
        args.append(sinks.astype(F32))
    n_out = n_pairs * (2 if with_lse else 1)
    out = pl.pallas_call(
        functools.partial(_band_attn_kernel, n_pairs=n_pairs, kv_pairs=kv_pairs, heads=tuple(heads),
                          dilation=dilation, nblk=nblk, with_sink=sinks is not None, with_lse=with_lse),
        grid=(s_, t_ // blk),
        in_specs=in_specs,
        out_specs=[pl.BlockSpec((None, blk, LANES), lambda n, b: (n, b, 0))] * n_out,
        out_shape=[jax.ShapeDtypeStruct((s_, t_, LANES), F32)] * n_out,
        compiler_params=_cparams("parallel", "arbitrary"),
        name="band_attn",
    )(*args)
    return (out[:n_pairs], out[n_pairs:]) if with_lse else out


def _decode_attn_kernel(*refs, nb, hkv, dilation, with_sink):
    it = iter(refs)
    q_ref, kv_ref, new_ref = next(it), next(it), next(it)
    sink_ref = next(it) if with_sink else None
    o_ref, lse_ref = next(it), next(it)
    rows = kv_ref.shape[-1]
    gq = q_ref.shape[2]
    pos = lax.broadcasted_iota(jnp.int32, (1, rows), 1)
    bias = jnp.where((pos & (dilation - 1)) == 0, 0.0, NEG)
    scale = HEAD_DIM ** -0.5

    def per_seq(i, c):
        qs = [q_ref[i, h] * scale for h in range(hkv)]
        new = new_ref[i]
        s = jnp.concatenate([_dot(qs[h].astype(BF16), kv_ref[i, 0, h].astype(BF16)) for h in range(hkv)], axis=0)
        s = s + bias
        sn = jnp.concatenate([jnp.sum(qs[h] * new[0, h:h + 1, :], axis=-1, keepdims=True) for h in range(hkv)],
                             axis=0)
        m = jnp.maximum(jnp.max(s, axis=-1, keepdims=True), sn)
        e, en = jnp.exp(s - m), jnp.exp(sn - m)
        lse = m + jnp.log(jnp.sum(e, axis=-1, keepdims=True) + en)
        if with_sink:
            sk = sink_ref[...]
            mx = jnp.maximum(lse, sk)
            lse = mx + jnp.log(jnp.exp(lse - mx) + jnp.exp(sk - mx))
        w = jnp.exp(m - lse)
        p, pn = (e * w).astype(BF16), en * w
        for h in range(hkv):
            hs = slice(h * gq, (h + 1) * gq)
            o = lax.dot_general(p[hs], kv_ref[i, 1, h].astype(BF16), (((1,), (1,)), ((), ())),
                                preferred_element_type=F32)
            o_ref[i, h] = o + pn[hs] * new[1, h:h + 1, :]
            lse_ref[i, h] = jnp.broadcast_to(lse[hs], (gq, HEAD_DIM))
        return c

    lax.fori_loop(0, nb, per_seq, 0, unroll=2 if nb % 2 == 0 else 1)


def decode_attn(q, cache, layer, new_kv, *, dilation, sinks=None, block_bytes=4 * 1024 * 1024):
    n, hkv, grp, dh = q.shape
    rows = cache.shape[2]
    gq = -(-grp // 8) * 8
    assert dilation & (dilation - 1) == 0
    nb = max(1, min(n, block_bytes // (2 * hkv * dh * rows * 4)))
    while n % nb:
        nb -= 1
    pad_g = lambda a: jnp.pad(a, [(0, 0)] * (a.ndim - 2) + [(0, gq - grp), (0, 0)])
    cache_t = cache.transpose(0, 1, 3, 4, 5, 2)
    in_specs = [pl.BlockSpec((nb, hkv, gq, dh), lambda i: (i, 0, 0, 0)),
                pl.BlockSpec((None, nb, 2, hkv, dh, rows), lambda i: (layer, i, 0, 0, 0, 0)),
                pl.BlockSpec((nb, 2, hkv, dh), lambda i: (i, 0, 0, 0))]
    args = [pad_g(q), cache_t, new_kv]
    if sinks is not None:
        in_specs.append(pl.BlockSpec((hkv * gq, 1), lambda i: (0, 0)))
        args.append(pad_g(sinks.astype(F32)[:, :, None]).reshape(hkv * gq, 1))
    out_spec = pl.BlockSpec((nb, hkv, gq, dh), lambda i: (i, 0, 0, 0))
    out_sds = jax.ShapeDtypeStruct((n, hkv, gq, dh), F32)
    o, lse = pl.pallas_call(
        functools.partial(_decode_attn_kernel, nb=nb, hkv=hkv, dilation=dilation, with_sink=sinks is not None),
        grid=(n // nb,),
        in_specs=in_specs,
        out_specs=[out_spec, out_spec],
        out_shape=[out_sds, out_sds],
        compiler_params=_cparams("parallel"),
        name="decode_attn",
    )(*args)
    return o[:, :, :grp], lse[:, :, :grp]


def _merge_kernel(*refs, nb_pairs, nc_pairs, n_groups):
    it = iter(refs)
    x_ref, ya_ref = next(it), next(it)
    lanes = lambda k: jnp.concatenate([next(it)[...] for _ in range(k)], axis=-1)
    ob = lanes(nb_pairs)
    ocs = [lanes(nc_pairs) for _ in range(n_groups)]
    lses = [lanes(nc_pairs) for _ in range(n_groups)]
    gmix_ref, wgate_ref = next(it), next(it)
    wglu_ref, bglu_ref, wba_ref, wbb_ref, wbc_ref, wout_ref, o_ref = (next(it) for _ in range(7))
    d = x_ref.shape[1]
    hn = _rmsnorm_bf16(x_ref[...], gmix_ref[...])
    gates = [jax.nn.sigmoid(_dot(hn, wgate_ref[:, b * d:(b + 1) * d])) for b in range(3)]
    g = jax.nn.gelu(ya_ref[...])
    oa = g * jax.nn.sigmoid(_dot(g.astype(BF16), wglu_ref[...]) + bglu_ref[...])
    mx = functools.reduce(jnp.maximum, lses)
    es = [jnp.exp(l - mx) for l in lses]
    oc = sum(e * o for e, o in zip(es, ocs)) / sum(es)
    merged = (gates[0] * _dot(oa.astype(BF16), wba_ref[...])
              + gates[1] * _dot(ob.astype(BF16), wbb_ref[...])
              + gates[2] * _dot(oc.astype(BF16), wbc_ref[...]))
    o_ref[...] = x_ref[...] + _dot(merged.astype(BF16), wout_ref[...])


def merge(x, gate_col, ya, ob, ocs, lses, lw, *, tm=512):
    rows, d = x.shape
    tm = _row_tile(rows, tm)
    row = lambda w: pl.BlockSpec((tm, w), lambda i: (i, 0))
    full = lambda a: pl.BlockSpec(a.shape, lambda i: (0, 0))
    weights = (lw['g_mix'].astype(F32).reshape(1, -1), lw['w_in'][:, gate_col:gate_col + 3 * d].astype(BF16),
               lw['w_glu'].astype(BF16), lw['b_glu'].astype(F32).reshape(1, -1), lw['w_branch_a'].astype(BF16),
               lw['w_branch_b'].astype(BF16), lw['w_branch_c'].astype(BF16), lw['w_out'].astype(BF16))
    pairs = list(ob) + [a for grp in ocs for a in grp] + [a for grp in lses for a in grp]
    return pl.pallas_call(
        functools.partial(_merge_kernel, nb_pairs=len(ob), nc_pairs=len(ocs[0]), n_groups=len(ocs)),
        grid=(rows // tm,),
        in_specs=[row(d), row(ya.shape[1])] + [row(LANES)] * len(pairs) + [full(w) for w in weights],
        out_specs=row(d),
        out_shape=jax.ShapeDtypeStruct((rows, d), F32),
        compiler_params=_cparams("parallel"),
        name="merge",
    )(x, ya, *pairs, *weights)


def _mlp_kernel(x_ref, g_ref, wup_ref, wdn_ref, o_ref, hn_ref, acc_ref):
    k = pl.program_id(1)

    @pl.when(k == 0)
    def _():
        hn_ref[...] = _rmsnorm_bf16(x_ref[...], g_ref[...])
        acc_ref[...] = jnp.zeros_like(acc_ref)

    h = jnp.maximum(_dot(hn_ref[...], wup_ref[...]), 0.0)
    acc_ref[...] += _dot((h * h).astype(BF16), wdn_ref[...])

    @pl.when(k == pl.num_programs(1) - 1)
    def _():
        o_ref[...] = x_ref[...] + acc_ref[...]


def mlp(x, g, w_up, w_down, *, tm=1024, tk=1024):
    rows, d = x.shape
    dff = w_up.shape[1]
    tm = _row_tile(rows, tm)
    assert dff % tk == 0
    return pl.pallas_call(
        _mlp_kernel,
        grid=(rows // tm, dff // tk),
        in_specs=[pl.BlockSpec((tm, d), lambda i, k: (i, 0)),
                  pl.BlockSpec((1, d), lambda i, k: (0, 0)),
                  pl.BlockSpec((d, tk), lambda i, k: (0, k)),
                  pl.BlockSpec((tk, d), lambda i, k: (k, 0))],
        out_specs=pl.BlockSpec((tm, d), lambda i, k: (i, 0)),
        out_shape=jax.ShapeDtypeStruct((rows, d), F32),
        scratch_shapes=[pltpu.VMEM((tm, d), BF16), pltpu.VMEM((tm, d), F32)],
        compiler_params=_cparams("parallel", "arbitrary"),
        name="mlp",
    )(x, g, w_up, w_down)


def _ple_kernel(x_ref, p_ref, g_ref, wg_ref, wp_ref, gf_ref, o_ref, *, final_norm):
    x = x_ref[...]
    gate = jax.nn.sigmoid(_dot(_rmsnorm_bf16(x, g_ref[...]), wg_ref[...]))
    y = x + gate * _dot(p_ref[...].astype(BF16), wp_ref[...])
    if final_norm:
        ms = jnp.mean(y * y, axis=-1, keepdims=True)
        y = y * lax.rsqrt(ms + EPS) * gf_ref[...]
    o_ref[...] = y


def ple(x, p, layer, g, w_gate, w_proj, g_final, *, final_norm, tm=512):
    rows, d = x.shape
    tm = _row_tile(rows, tm)
    row = lambda w: pl.BlockSpec((tm, w), lambda i: (i, 0))
    full = lambda a: pl.BlockSpec(a.shape, lambda i: (0, 0))
    return pl.pallas_call(
        functools.partial(_ple_kernel, final_norm=final_norm),
        grid=(rows // tm,),
        in_specs=[row(d), pl.BlockSpec((None, tm, p.shape[2]), lambda i: (layer, i, 0)),
                  full(g), full(w_gate), full(w_proj), full(g_final)],
        out_specs=row(d),
        out_shape=jax.ShapeDtypeStruct((rows, d), F32),
        compiler_params=_cparams("parallel"),
        name="ple",
    )(x, p, g, w_gate, w_proj, g_final)


def _column_offsets():
    ssm_w = 32 * SSM_GROUP
    swa_q, swa_kv, dil_w = SWA_Q_HEADS * HEAD_DIM, SWA_KV_HEADS * HEAD_DIM, DIL_HEADS * HEAD_DIM
    off = {'u': 0, 'swa_q': ssm_w, 'swa_k': ssm_w + swa_q, 'swa_v': ssm_w + swa_q + swa_kv}
    off['dil'] = ssm_w + swa_q + 2 * swa_kv
    off['gates'] = off['dil'] + 3 * len(DIL_PAIRS) * dil_w
    return off


def _prompt_mixer(z, n, t, lw):
    off = _column_offsets()
    rows = n * t
    z3 = z.reshape(n, t, -1)
    dil_w = DIL_HEADS * HEAD_DIM

    ya, fin = ssm_scan(z, n, t, lw)
    groups = fin.shape[-1] // SSM_STATE
    ssm_new = jnp.stack([fin[:, 0].reshape(n, groups, SSM_STATE), fin[:, 1].reshape(n, groups, SSM_STATE)], axis=-1)

    flat = lambda pairs: [a.reshape(rows, LANES) for a in pairs]
    ob = flat(band_attn(z3, q_col=off['swa_q'], k_col=off['swa_k'], v_col=off['swa_v'],
                        q_heads=SWA_Q_HEADS, kv_heads=SWA_KV_HEADS, sinks=lw['sinks']))
    keep = min(BAND, t)
    swa_new = z3[:, t - keep:, off['swa_k']:off['swa_k'] + 2 * SWA_KV_HEADS * HEAD_DIM].reshape(
        n, keep, 2, SWA_KV_HEADS, HEAD_DIM)

    ocs, lses, dil_new = [], [], []
    for gi, (win, dil) in enumerate(DIL_PAIRS):
        assert win == BAND * dil
        c0 = off['dil'] + 3 * dil_w * gi
        o, lse = band_attn(z3, q_col=c0, k_col=c0 + dil_w, v_col=c0 + 2 * dil_w,
                           q_heads=DIL_HEADS, kv_heads=DIL_HEADS, dilation=dil, with_lse=True)
        ocs.append(flat(o))
        lses.append(flat(lse))
        keep = min(win, t)
        dil_new.append(z3[:, t - keep:, c0 + dil_w:c0 + 3 * dil_w].reshape(n, keep, 2, DIL_HEADS, HEAD_DIM))
    return ya, ob, ocs, lses, (swa_new, dil_new[0], dil_new[1], dil_new[2], ssm_new)


def _sample_mixer(z, lw, cache):
    off = _column_offsets()
    n = z.shape[0]
    dil_w = DIL_HEADS * HEAD_DIM
    ssm_w = off['swa_q']
    layer = cache['layer']

    h0 = cache['ssm'].astype(F32)
    ya, s_r, s_i = ssm_step(z[:, :ssm_w], h0[..., 0].reshape(n, -1), h0[..., 1].reshape(n, -1), lw)
    ssm_new = jnp.stack([s_r.reshape(h0.shape[:-1]), s_i.reshape(h0.shape[:-1])], axis=-1)

    def attend(q_col, kv_col, q_heads, kv_heads, buf, dilation, sinks=None):
        grp = q_heads // kv_heads
        q = z[:, q_col:q_col + q_heads * HEAD_DIM].reshape(n, kv_heads, grp, HEAD_DIM)
        new_kv = z[:, kv_col:kv_col + 2 * kv_heads * HEAD_DIM].reshape(n, 2, kv_heads, HEAD_DIM)
        if sinks is not None:
            sinks = sinks.reshape(kv_heads, grp)
        o, lse = decode_attn(q, buf, layer, new_kv, dilation=dilation, sinks=sinks)
        pairs = lambda a: [a.reshape(n, q_heads * HEAD_DIM)[:, j * LANES:(j + 1) * LANES]
                           for j in range(q_heads * HEAD_DIM // LANES)]
        return pairs(o), pairs(lse), new_kv[:, None]

    ob, _, swa_new = attend(off['swa_q'], off['swa_k'], SWA_Q_HEADS, SWA_KV_HEADS, cache['swa'], 1,
                            sinks=lw['sinks'])
    ocs, lses, dil_new = [], [], []
    for gi, (win, dil) in enumerate(DIL_PAIRS):
        c0 = off['dil'] + 3 * dil_w * gi
        o, lse, new = attend(c0, c0 + dil_w, DIL_HEADS, DIL_HEADS, cache['dil'][gi], dil)
        ocs.append(o)
        lses.append(lse)
        dil_new.append(new)
    return ya, ob, ocs, lses, (swa_new, dil_new[0], dil_new[1], dil_new[2], ssm_new)


def _layer(x, p, layer, lw, cache, n, t, g_final, final_norm):
    off = _column_offsets()
    row1 = lambda a: a.astype(F32).reshape(1, -1)
    z = norm_matmul(x, row1(lw['g_mix']), lw['w_in'][:, :off['gates']].astype(BF16))
    if cache is None:
        ya, ob, ocs, lses, st = _prompt_mixer(z, n, t, lw)
    else:
        assert t == 1
        win_rows = [c.shape[2] for c in (cache['swa'],) + tuple(cache['dil'])]
        assert win_rows == [BAND] + [w for w, _ in DIL_PAIRS], win_rows
        ya, ob, ocs, lses, st = _sample_mixer(z, lw, cache)
    x = merge(x, off['gates'], ya, ob, ocs, lses, lw)
    x = mlp(x, row1(lw['g_mlp']), lw['w_up'].astype(BF16), lw['w_down'].astype(BF16))
    x = ple(x, p, layer, row1(lw['g_ple']), lw['w_ple_gate'].astype(BF16), lw['w_ple_proj'].astype(BF16),
            row1(g_final), final_norm=final_norm)
    return x, st


def kernel(x_prompt, x_sample, cache_swa_kv, cache_dil_d1_kv, cache_dil_d4_kv, cache_dil_d16_kv, state_ssm, p_prompt, p_sample, w_in, g_mix, ssm_a_re, ssm_a_im, ssm_log_dt, ssm_b_re, ssm_b_im, ssm_c_re, ssm_c_im, ssm_d, w_glu, b_glu, attn_sinks, w_branch_a, w_branch_b, w_branch_c, w_out, g_mlp, w_up, w_down, g_ple, w_ple_gate, w_ple_proj, g_final):
    depth = w_in.shape[0]
    bp, tp, d = x_prompt.shape
    bs, ts, _ = x_sample.shape
    yp = x_prompt.reshape(bp * tp, d)
    ys = x_sample.reshape(bs * ts, d)
    st_p, st_s = [], []
    for l in range(depth):
        lw = {'w_in': w_in[l], 'g_mix': g_mix[l], 'a_re': ssm_a_re[l], 'a_im': ssm_a_im[l],
              'log_dt': ssm_log_dt[l], 'b_re': ssm_b_re[l], 'b_im': ssm_b_im[l], 'c_re': ssm_c_re[l],
              'c_im': ssm_c_im[l], 'd': ssm_d[l], 'w_glu': w_glu[l], 'b_glu': b_glu[l], 'sinks': attn_sinks[l],
              'w_branch_a': w_branch_a[l], 'w_branch_b': w_branch_b[l], 'w_branch_c': w_branch_c[l],
              'w_out': w_out[l], 'g_mlp': g_mlp[l], 'w_up': w_up[l], 'w_down': w_down[l], 'g_ple': g_ple[l],
              'w_ple_gate': w_ple_gate[l], 'w_ple_proj': w_ple_proj[l]}
        cache_l = {'layer': l, 'swa': cache_swa_kv,
                   'dil': (cache_dil_d1_kv, cache_dil_d4_kv, cache_dil_d16_kv),
                   'ssm': state_ssm[l]}
        last = l == depth - 1
        yp, sp = _layer(yp, p_prompt.reshape(depth, bp * tp, -1), l, lw, None, bp, tp, g_final, last)
        ys, ss = _layer(ys, p_sample.reshape(depth, bs * ts, -1), l, lw, cache_l, bs, ts, g_final, last)
        st_p.append(sp)
        st_s.append(ss)
    stk = lambda sts, i: jnp.stack([s[i] for s in sts])
    return (yp.reshape(bp, tp, d), ys.reshape(bs, ts, d),
            stk(st_p, 0), stk(st_p, 1), stk(st_p, 2), stk(st_p, 3), stk(st_p, 4),
            stk(st_s, 0), stk(st_s, 1), stk(st_s, 2), stk(st_s, 3), stk(st_s, 4))
```

```python
import functools

import jax
import jax.numpy as jnp
from jax import lax
from jax.experimental import pallas as pl
from jax.experimental.pallas import tpu as pltpu

HEAD_DIM = 64
SSM_GROUP = 16
SSM_STATE = 64
SWA_Q_HEADS = 8
SWA_KV_HEADS = 2
DIL_HEADS = 4
DIL_PAIRS = ((128, 1), (512, 4), (2048, 16))
BAND = 128
SSM_CHUNK = 16
SSM_TILE_GROUPS = 16
EPS = 1e-6
LANES = 128
NEG = -1e30
VMEM_LIMIT = 48 * 1024 * 1024
BF16 = jnp.bfloat16
F32 = jnp.float32


def _cparams(*sem):
    return pltpu.CompilerParams(dimension_semantics=sem, vmem_limit_bytes=VMEM_LIMIT)


def _rmsnorm_bf16(xf, g):
    ms = jnp.mean(xf * xf, axis=-1, keepdims=True)
    return (xf * lax.rsqrt(ms + EPS) * g).astype(BF16)


def _dot(a, b):
    return jnp.dot(a, b, preferred_element_type=F32)


def _row_tile(rows, pref):
    t = min(rows, pref)
    assert rows % t == 0, (rows, pref)
    return t


def _norm_matmul_kernel(x_ref, g_ref, w_ref, o_ref, hn_ref):
    @pl.when(pl.program_id(1) == 0)
    def _():
        hn_ref[...] = _rmsnorm_bf16(x_ref[...], g_ref[...])

    o_ref[...] = _dot(hn_ref[...], w_ref[...])


def norm_matmul(x, g, w, *, tm=2048, tn=512):
    rows, d = x.shape
    cols = w.shape[1]
    tm = _row_tile(rows, tm)
    assert cols % tn == 0
    return pl.pallas_call(
        _norm_matmul_kernel,
        grid=(rows // tm, cols // tn),
        in_specs=[pl.BlockSpec((tm, d), lambda i, j: (i, 0)),
                  pl.BlockSpec((1, d), lambda i, j: (0, 0)),
                  pl.BlockSpec((d, tn), lambda i, j: (0, j))],
        out_specs=pl.BlockSpec((tm, tn), lambda i, j: (i, j)),
        out_shape=jax.ShapeDtypeStruct((rows, cols), F32),
        scratch_shapes=[pltpu.VMEM((tm, d), BF16)],
        compiler_params=_cparams("parallel", "arbitrary"),
        name="norm_matmul",
    )(x, g, w)


def _ssm_discretize(lw):
    a_re, a_im = lw['a_re'].astype(F32), lw['a_im'].astype(F32)
    dt = jnp.exp(lw['log_dt'].astype(F32))[:, None]
    mag = jnp.exp(a_re * dt)
    lam_r, lam_i = mag * jnp.cos(a_im * dt), mag * jnp.sin(a_im * dt)
    den = a_re * a_re + a_im * a_im
    zr = ((lam_r - 1.0) * a_re + lam_i * a_im) / den
    zi = (lam_i * a_re - (lam_r - 1.0) * a_im) / den
    b_re, b_im = lw['b_re'].astype(F32), lw['b_im'].astype(F32)
    bb_r = zr[..., None] * b_re - zi[..., None] * b_im
    bb_i = zr[..., None] * b_im + zi[..., None] * b_re
    return lam_r, lam_i, bb_r, bb_i


def _ssm_scan_operands(lw):
    lam_r, lam_i, bb_r, bb_i = _ssm_discretize(lw)
    g, p, h = bb_r.shape
    tg = SSM_TILE_GROUPS
    eye = jnp.eye(tg, dtype=F32)
    tile_in = lambda b: (b.reshape(g // tg, tg, p, h).transpose(0, 1, 3, 2)[:, :, :, None, :]
                         * eye[None, :, None, :, None]).reshape(g // tg, tg * h, tg * p)
    tile_out = lambda c: (c.reshape(g // tg, tg, h, p).transpose(0, 1, 3, 2)[:, :, :, None, :]
                          * eye[None, :, None, :, None]).reshape(g // tg, tg * p, tg * h)
    pw_r, pw_i = [lam_r], [lam_i]
    for _ in range(SSM_CHUNK - 1):
        pr, pi = pw_r[-1], pw_i[-1]
        pw_r.append(pr * lam_r - pi * lam_i)
        pw_i.append(pr * lam_i + pi * lam_r)
    pw = jnp.stack([jnp.stack(pw_r).reshape(SSM_CHUNK, g * p), jnp.stack(pw_i).reshape(SSM_CHUNK, g * p)])
    return (tile_in(bb_r).astype(BF16), tile_in(bb_i).astype(BF16),
            tile_out(lw['c_re'].astype(F32)).astype(BF16), tile_out(lw['c_im'].astype(F32)).astype(BF16),
            pw, lw['d'].astype(F32).reshape(1, g * h))


def _ssm_scan_kernel(*refs, n_ucols, lane_block):
    u_refs = refs[:n_ucols]
    (bbr_ref, bbi_ref, ccr_ref, cci_ref, pw_ref, d_ref, y_ref, fin_ref, xr, xi, cr, ci, carry, yperm) = refs[n_ucols:]
    tiles, tin, tst = bbr_ref.shape
    tb, sw = xr.shape
    chunks = tb // SSM_CHUNK

    @pl.when(pl.program_id(1) == 0)
    def _():
        carry[...] = jnp.zeros_like(carry)

    u = jnp.concatenate(
        [jnp.concatenate([r[pl.ds(j, chunks, stride=SSM_CHUNK), :] for j in range(SSM_CHUNK)], axis=0)
         for r in u_refs], axis=-1)
    ub = u.astype(BF16)
    for k in range(tiles):
        xr[:, k * tst:(k + 1) * tst] = _dot(ub[:, k * tin:(k + 1) * tin], bbr_ref[k])
        xi[:, k * tst:(k + 1) * tst] = _dot(ub[:, k * tin:(k + 1) * tin], bbi_ref[k])

    step_rows = lambda j: pl.ds(j * chunks, chunks)
    for lb in range(sw // lane_block):
        cols = pl.ds(lb * lane_block, lane_block)
        lr, li = pw_ref[0, 0:1, cols], pw_ref[1, 0:1, cols]
        sr, si = xr[step_rows(0), cols], xi[step_rows(0), cols]
        for j in range(1, SSM_CHUNK):
            nr = lr * sr - li * si + xr[step_rows(j), cols]
            ni = lr * si + li * sr + xi[step_rows(j), cols]
            xr[step_rows(j), cols] = nr
            xi[step_rows(j), cols] = ni
            sr, si = nr, ni

    lcr, lci = pw_ref[0, SSM_CHUNK - 1:SSM_CHUNK, :], pw_ref[1, SSM_CHUNK - 1:SSM_CHUNK, :]

    def chunk_step(c, state):
        kr, ki = state
        cr[pl.ds(c, 1), :] = kr
        ci[pl.ds(c, 1), :] = ki
        last = (SSM_CHUNK - 1) * chunks + c
        return (lcr * kr - lci * ki + xr[pl.ds(last, 1), :], lcr * ki + lci * kr + xi[pl.ds(last, 1), :])

    kr, ki = lax.fori_loop(0, chunks, chunk_step, (carry[0:1, :], carry[1:2, :]))
    carry[0:1, :] = kr
    carry[1:2, :] = ki
    fin_ref[0:1, :] = kr
    fin_ref[1:2, :] = ki

    for lb in range(sw // lane_block):
        cols = pl.ds(lb * lane_block, lane_block)
        ckr, cki = cr[:, cols], ci[:, cols]
        for j in range(SSM_CHUNK):
            pr, pi = pw_ref[0, j:j + 1, cols], pw_ref[1, j:j + 1, cols]
            xr[step_rows(j), cols] = xr[step_rows(j), cols] + (pr * ckr - pi * cki)
            xi[step_rows(j), cols] = xi[step_rows(j), cols] + (pr * cki + pi * ckr)

    per_tile = tin // LANES
    for k in range(tiles):
        st = pl.ds(k * tst, tst)
        y = (_dot(xr[:, st].astype(BF16), ccr_ref[k]) - _dot(xi[:, st].astype(BF16), cci_ref[k])
             + u[:, k * tin:(k + 1) * tin] * d_ref[:, k * tin:(k + 1) * tin])
        for l in range(per_tile):
            yperm[k * per_tile + l] = y[:, l * LANES:(l + 1) * LANES]
    for c in range(chunks):
        for l in range(n_ucols):
            y_ref[c * SSM_CHUNK:(c + 1) * SSM_CHUNK, l * LANES:(l + 1) * LANES] = (
                yperm.at[l][pl.ds(c, SSM_CHUNK, stride=chunks), :])


def ssm_scan(z, n, t, lw, *, tb=512, lane_block=256):
    ops = _ssm_scan_operands(lw)
    bbr = ops[0]
    tiles, tin, tst = bbr.shape
    w, sw = tiles * tin, tiles * tst
    tb = _row_tile(t, tb)
    assert tb % (8 * SSM_CHUNK) == 0 and sw % lane_block == 0 and tin % LANES == 0
    nt = t // tb
    n_ucols = w // LANES
    ucol = lambda l: pl.BlockSpec((tb, LANES), lambda i, k: (i * nt + k, l))
    full = lambda a: pl.BlockSpec(a.shape, lambda i, k: (0,) * a.ndim)
    return pl.pallas_call(
        functools.partial(_ssm_scan_kernel, n_ucols=n_ucols, lane_block=lane_block),
        grid=(n, nt),
        in_specs=[ucol(l) for l in range(n_ucols)] + [full(a) for a in ops],
        out_specs=[pl.BlockSpec((tb, w), lambda i, k: (i * nt + k, 0)),
                   pl.BlockSpec((None, 2, sw), lambda i, k: (i, 0, 0))],
        out_shape=[jax.ShapeDtypeStruct((n * t, w), F32), jax.ShapeDtypeStruct((n, 2, sw), F32)],
        scratch_shapes=[pltpu.VMEM((tb, sw), F32), pltpu.VMEM((tb, sw), F32),
                        pltpu.VMEM((tb // SSM_CHUNK, sw), F32), pltpu.VMEM((tb // SSM_CHUNK, sw), F32),
                        pltpu.VMEM((2, sw), F32), pltpu.VMEM((n_ucols, tb, LANES), F32)],
        compiler_params=_cparams("parallel", "arbitrary"),
        name="ssm_scan",
    )(*([z] * n_ucols), *ops)


def _ssm_step_kernel(u_ref, h0r_ref, h0i_ref, bbr_ref, bbi_ref, ccr_ref, cci_ref, lr_ref, li_ref, d_ref,
                     y_ref, sr_ref, si_ref):
    u = u_ref[...]
    ub = u.astype(BF16)
    lr, li = lr_ref[...], li_ref[...]
    h0r, h0i = h0r_ref[...], h0i_ref[...]
    sr = _dot(ub, bbr_ref[...]) + (lr * h0r - li * h0i)
    si = _dot(ub, bbi_ref[...]) + (lr * h0i + li * h0r)
    sr_ref[...] = sr
    si_ref[...] = si
    y_ref[...] = _dot(sr.astype(BF16), ccr_ref[...]) - _dot(si.astype(BF16), cci_ref[...]) + u * d_ref[...]


def ssm_step(u, h0r, h0i, lw):
    lam_r, lam_i, bb_r, bb_i = _ssm_discretize(lw)
    g, p, h = bb_r.shape
    eye = jnp.eye(g, dtype=F32)
    dense_in = lambda b: (b.transpose(0, 2, 1)[:, :, None, :] * eye[:, None, :, None]).reshape(g * h, g * p)
    dense_out = lambda c: (c.transpose(0, 2, 1)[:, :, None, :] * eye[:, None, :, None]).reshape(g * p, g * h)
    n = u.shape[0]
    args = (u, h0r, h0i, dense_in(bb_r).astype(BF16), dense_in(bb_i).astype(BF16),
            dense_out(lw['c_re'].astype(F32)).astype(BF16), dense_out(lw['c_im'].astype(F32)).astype(BF16),
            lam_r.reshape(1, g * p), lam_i.reshape(1, g * p), lw['d'].astype(F32).reshape(1, g * h))
    full = lambda a: pl.BlockSpec(a.shape, lambda i: (0, 0))
    return pl.pallas_call(
        _ssm_step_kernel,
        grid=(1,),
        in_specs=[full(a) for a in args],
        out_specs=[pl.BlockSpec((n, g * h), lambda i: (0, 0)),
                   pl.BlockSpec((n, g * p), lambda i: (0, 0)),
                   pl.BlockSpec((n, g * p), lambda i: (0, 0))],
        out_shape=[jax.ShapeDtypeStruct((n, g * h), F32),
                   jax.ShapeDtypeStruct((n, g * p), F32),
                   jax.ShapeDtypeStruct((n, g * p), F32)],
        compiler_params=_cparams("arbitrary"),
        name="ssm_step",
    )(*args)


def _band_attn_kernel(*refs, n_pairs, kv_pairs, heads, dilation, nblk, with_sink, with_lse):
    it = iter(refs)
    q_refs = [next(it) for _ in range(n_pairs)]
    kv_refs = [[next(it) for _ in range(4)] for _ in range(kv_pairs)]
    sink_ref = next(it) if with_sink else None
    o_refs = [next(it) for _ in range(n_pairs)]
    lse_refs = [next(it) for _ in range(n_pairs)] if with_lse else None
    b = pl.program_id(1)
    kj = lax.broadcasted_iota(jnp.int32, (2 * BAND, BAND), 0)
    qi = lax.broadcasted_iota(jnp.int32, (2 * BAND, BAND), 1)
    in_band = (kj >= qi) & (kj <= qi + BAND)
    bias_rest = jnp.where(in_band, 0.0, NEG)
    bias_first = jnp.where(in_band & ((kj >= BAND) | (b > 0)), 0.0, NEG)
    lane = lax.broadcasted_iota(jnp.int32, (1, LANES), 1)
    lane_half = [(lane < HEAD_DIM).astype(F32), (lane >= HEAD_DIM).astype(F32)]
    row = lax.broadcasted_iota(jnp.int32, (LANES, 1), 0)
    row_half = [row < HEAD_DIM, row >= HEAD_DIM]

    members = {}
    for j in range(n_pairs):
        for half in (0, 1):
            members.setdefault(heads[j][half], []).append((half, j))

    def block(r, i):
        rows = pl.ds(r, BAND, stride=dilation) if dilation > 1 else pl.ds(i * BAND, BAND)
        bias = bias_first if i == 0 else bias_rest

        def with_prev(prev_ref, cur_ref):
            if i > 0:
                prev = cur_ref[pl.ds((i - 1) * BAND, BAND), :]
            else:
                prev = prev_ref[rows, :] if dilation > 1 else prev_ref[...]
            return jnp.concatenate([prev, cur_ref[rows, :]], axis=0)

        q2 = [q_refs[j][rows, :] * (HEAD_DIM ** -0.5) for j in range(n_pairs)]
        ot = [jnp.zeros((LANES, BAND), F32) for _ in range(n_pairs)]
        lset = [jnp.zeros((LANES, BAND), F32) for _ in range(n_pairs)]
        base = {}
        for (pair, rot), group in members.items():
            group = sorted(group)
            if pair not in base:
                kp_ref, kc_ref, vp_ref, vc_ref = kv_refs[pair]
                base[pair] = (with_prev(kp_ref, kc_ref), with_prev(vp_ref, vc_ref).T)
            k2, vt = base[pair]
            if rot:
                k2 = pltpu.roll(k2, HEAD_DIM, 1)
                vt = jnp.concatenate([vt[HEAD_DIM:], vt[:HEAD_DIM]], axis=0)
            nh = len(group)
            qm = jnp.concatenate([(q2[j] * lane_half[half]).astype(BF16) for half, j in group], axis=0)
            st = lax.dot_general(k2.astype(BF16), qm, (((1,), (1,)), ((), ())), preferred_element_type=F32)
            st = st + jnp.concatenate([bias] * nh, axis=1)
            m = jnp.max(st, axis=0, keepdims=True)
            e = jnp.exp(st - m)
            lse = m + jnp.log(jnp.sum(e, axis=0, keepdims=True))
            if with_sink:
                sk = jnp.concatenate([jnp.full((1, BAND), sink_ref[2 * j + half], F32) for half, j in group], axis=1)
                mx = jnp.maximum(lse, sk)
                lse = mx + jnp.log(jnp.exp(lse - mx) + jnp.exp(sk - mx))
            pt = (e * jnp.exp(m - lse)).astype(BF16)
            for half in (0, 1):
                idx = [g for g, (hf, _) in enumerate(group) if hf == half]
                if not idx:
                    continue
                vb = jnp.where(row_half[half], vt, 0.0).astype(BF16)
                res = _dot(vb, pt[:, idx[0] * BAND:(idx[-1] + 1) * BAND])
                for n_, g in enumerate(idx):
                    j = group[g][1]
                    ot[j] = ot[j] + res[:, n_ * BAND:(n_ + 1) * BAND]
                    if with_lse:
                        lset[j] = jnp.where(row_half[half], lse[:, g * BAND:(g + 1) * BAND], lset[j])
        for j in range(n_pairs):
            o_refs[j][rows, :] = ot[j].T
            if with_lse:
                lse_refs[j][rows, :] = lset[j].T

    if dilation == 1:
        for i in range(nblk):
            block(0, i)
    else:
        def body(r, c):
            block(r, 0)
            return c
        lax.fori_loop(0, dilation, body, 0, unroll=min(dilation, nblk))


def band_attn(src, *, q_col, k_col, v_col, q_heads, kv_heads, dilation=1, sinks=None, with_lse=False, nblk=8):
    s_, t_, _ = src.shape
    blk = BAND * (dilation if dilation > 1 else nblk)
    assert t_ % blk == 0 and q_heads % 2 == 0 and kv_heads % 2 == 0
    grp = q_heads // kv_heads
    n_pairs = q_heads // 2
    kv_pairs = kv_heads // 2
    assert q_col % LANES == 0 and k_col % LANES == 0 and v_col % LANES == 0
    heads = []
    for j in range(n_pairs):
        pair_heads = []
        for half in (0, 1):
            kvh = (2 * j + half) // grp
            pair_heads.append((kvh // 2, (kvh % 2) != half))
        heads.append(tuple(pair_heads))
    cur = lambda col: pl.BlockSpec((None, blk, LANES), lambda n, b: (n, b, col // LANES))
    if dilation > 1:
        prev = lambda col: pl.BlockSpec((None, blk, LANES), lambda n, b: (n, jnp.maximum(b - 1, 0), col // LANES))
    else:
        prev = lambda col: pl.BlockSpec((None, BAND, LANES),
                                        lambda n, b: (n, jnp.maximum(b * nblk - 1, 0), col // LANES))
    in_specs = [cur(q_col + j * LANES) for j in range(n_pairs)]
    for kp in range(kv_pairs):
        kc, vc = k_col + kp * LANES, v_col + kp * LANES
        in_specs += [prev(kc), cur(kc), prev(vc), cur(vc)]
    args = [src] * len(in_specs)
    if sinks is not None:
        in_specs.append(pl.BlockSpec(memory_space=pltpu.SMEM))
        args.append(sinks.astype(F32))
    n_out = n_pairs * (2 if with_lse else 1)
    out = pl.pallas_call(
        functools.partial(_band_attn_kernel, n_pairs=n_pairs, kv_pairs=kv_pairs, heads=tuple(heads),
                          dilation=dilation, nblk=nblk, with_sink=sinks is not None, with_lse=with_lse),
        grid=(s_, t_ // blk),
        in_specs=in_specs,
        out_specs=[pl.BlockSpec((None, blk, LANES), lambda n, b: (n, b, 0))] * n_out,
        out_shape=[jax.ShapeDtypeStruct((s_, t_, LANES), F32)] * n_out,
        compiler_params=_cparams("parallel", "arbitrary"),
        name="band_attn",
    )(*args)
    return (out[:n_pairs], out[n_pairs:]) if with_lse else out


def _decode_attn_kernel(*refs, nb, hkv, dilation, with_sink):
    it = iter(refs)
    q_ref, kv_ref, new_ref = next(it), next(it), next(it)
    sink_ref = next(it) if with_sink else None
    o_ref, lse_ref = next(it), next(it)
    rows = kv_ref.shape[-1]
    gq = q_ref.shape[2]
    pos = lax.broadcasted_iota(jnp.int32, (1, rows), 1)
    bias = jnp.where((pos & (dilation - 1)) == 0, 0.0, NEG)
    scale = HEAD_DIM ** -0.5

    sb = max(d for d in (1, 2, 4) if nb % d == 0)
    heads = [(j, h) for j in range(sb) for h in range(hkv)]

    def per_group(it, c):
        i0 = it * sb
        qs = [q_ref[i0 + j, h] * scale for j, h in heads]
        new = [new_ref[i0 + j] for j in range(sb)]
        s = jnp.concatenate([_dot(q.astype(BF16), kv_ref[i0 + j, 0, h].astype(BF16))
                             for q, (j, h) in zip(qs, heads)], axis=0) + bias
        sn = jnp.concatenate([jnp.sum(q * new[j][0, h:h + 1, :], axis=-1, keepdims=True)
                              for q, (j, h) in zip(qs, heads)], axis=0)
        m = jnp.maximum(jnp.max(s, axis=-1, keepdims=True), sn)
        e, en = jnp.exp(s - m), jnp.exp(sn - m)
        lse = m + jnp.log(jnp.sum(e, axis=-1, keepdims=True) + en)
        if with_sink:
            sk = jnp.concatenate([sink_ref[...]] * sb, axis=0)
            mx = jnp.maximum(lse, sk)
            lse = mx + jnp.log(jnp.exp(lse - mx) + jnp.exp(sk - mx))
        w = jnp.exp(m - lse)
        p, pn = (e * w).astype(BF16), en * w
        for n_, (j, h) in enumerate(heads):
            hs = slice(n_ * gq, (n_ + 1) * gq)
            o = lax.dot_general(p[hs], kv_ref[i0 + j, 1, h].astype(BF16), (((1,), (1,)), ((), ())),
                                preferred_element_type=F32)
            o_ref[i0 + j, h] = o + pn[hs] * new[j][1, h:h + 1, :]
            lse_ref[i0 + j, h] = jnp.broadcast_to(lse[hs], (gq, HEAD_DIM))
        return c

    lax.fori_loop(0, nb // sb, per_group, 0)


def decode_attn(q, cache, layer, new_kv, *, dilation, sinks=None, block_bytes=4 * 1024 * 1024):
    n, hkv, grp, dh = q.shape
    rows = cache.shape[2]
    gq = -(-grp // 8) * 8
    assert dilation & (dilation - 1) == 0
    nb = max(1, min(n, block_bytes // (2 * hkv * dh * rows * 4)))
    while n % nb:
        nb -= 1
    pad_g = lambda a: jnp.pad(a, [(0, 0)] * (a.ndim - 2) + [(0, gq - grp), (0, 0)])
    cache_t = cache.transpose(0, 1, 3, 4, 5, 2)
    in_specs = [pl.BlockSpec((nb, hkv, gq, dh), lambda i: (i, 0, 0, 0)),
                pl.BlockSpec((None, nb, 2, hkv, dh, rows), lambda i: (layer, i, 0, 0, 0, 0)),
                pl.BlockSpec((nb, 2, hkv, dh), lambda i: (i, 0, 0, 0))]
    args = [pad_g(q), cache_t, new_kv]
    if sinks is not None:
        in_specs.append(pl.BlockSpec((hkv * gq, 1), lambda i: (0, 0)))
        args.append(pad_g(sinks.astype(F32)[:, :, None]).reshape(hkv * gq, 1))
    out_spec = pl.BlockSpec((nb, hkv, gq, dh), lambda i: (i, 0, 0, 0))
    out_sds = jax.ShapeDtypeStruct((n, hkv, gq, dh), F32)
    o, lse = pl.pallas_call(
        functools.partial(_decode_attn_kernel, nb=nb, hkv=hkv, dilation=dilation, with_sink=sinks is not None),
        grid=(n // nb,),
        in_specs=in_specs,
        out_specs=[out_spec, out_spec],
        out_shape=[out_sds, out_sds],
        compiler_params=_cparams("parallel"),
        name="decode_attn",
    )(*args)
    return o[:, :, :grp], lse[:, :, :grp]


def _merge_kernel(*refs, nb_pairs, nc_pairs, n_groups):
    it = iter(refs)
    x_ref, ya_ref = next(it), next(it)
    lanes = lambda k: jnp.concatenate([next(it)[...] for _ in range(k)], axis=-1)
    ob = lanes(nb_pairs)
    ocs = [lanes(nc_pairs) for _ in range(n_groups)]
    lses = [lanes(nc_pairs) for _ in range(n_groups)]
    gmix_ref, wgate_ref = next(it), next(it)
    wglu_ref, bglu_ref, wba_ref, wbb_ref, wbc_ref, wout_ref, o_ref = (next(it) for _ in range(7))
    d = x_ref.shape[1]
    hn = _rmsnorm_bf16(x_ref[...], gmix_ref[...])
    gates = [jax.nn.sigmoid(_dot(hn, wgate_ref[:, b * d:(b + 1) * d])) for b in range(3)]
    g = jax.nn.gelu(ya_ref[...])
    oa = g * jax.nn.sigmoid(_dot(g.astype(BF16), wglu_ref[...]) + bglu_ref[...])
    mx = functools.reduce(jnp.maximum, lses)
    es = [jnp.exp(l - mx) for l in lses]
    oc = sum(e * o for e, o in zip(es, ocs)) / sum(es)
    merged = (gates[0] * _dot(oa.astype(BF16), wba_ref[...])
              + gates[1] * _dot(ob.astype(BF16), wbb_ref[...])
              + gates[2] * _dot(oc.astype(BF16), wbc_ref[...]))
    o_ref[...] = x_ref[...] + _dot(merged.astype(BF16), wout_ref[...])


def merge(x, gate_col, ya, ob, ocs, lses, lw, *, tm=512):
    rows, d = x.shape
    tm = _row_tile(rows, tm)
    row = lambda w: pl.BlockSpec((tm, w), lambda i: (i, 0))
    full = lambda a: pl.BlockSpec(a.shape, lambda i: (0, 0))
    weights = (lw['g_mix'].astype(F32).reshape(1, -1), lw['w_in'][:, gate_col:gate_col + 3 * d].astype(BF16),
               lw['w_glu'].astype(BF16), lw['b_glu'].astype(F32).reshape(1, -1), lw['w_branch_a'].astype(BF16),
               lw['w_branch_b'].astype(BF16), lw['w_branch_c'].astype(BF16), lw['w_out'].astype(BF16))
    pairs = list(ob) + [a for grp in ocs for a in grp] + [a for grp in lses for a in grp]
    return pl.pallas_call(
        functools.partial(_merge_kernel, nb_pairs=len(ob), nc_pairs=len(ocs[0]), n_groups=len(ocs)),
        grid=(rows // tm,),
        in_specs=[row(d), row(ya.shape[1])] + [row(LANES)] * len(pairs) + [full(w) for w in weights],
        out_specs=row(d),
        out_shape=jax.ShapeDtypeStruct((rows, d), F32),
        compiler_params=_cparams("parallel"),
        name="merge",
    )(x, ya, *pairs, *weights)


def _mlp_ple_kernel(x_ref, g_ref, wup_ref, wdn_ref, p_ref, gp_ref, wg_ref, wp_ref, gf_ref, o_ref, hn_ref, acc_ref,
                    *, final_norm):
    k = pl.program_id(1)

    @pl.when(k == 0)
    def _():
        hn_ref[...] = _rmsnorm_bf16(x_ref[...], g_ref[...])
        acc_ref[...] = jnp.zeros_like(acc_ref)

    h = jnp.maximum(_dot(hn_ref[...], wup_ref[...]), 0.0)
    acc_ref[...] += _dot((h * h).astype(BF16), wdn_ref[...])

    @pl.when(k == pl.num_programs(1) - 1)
    def _():
        x1 = x_ref[...] + acc_ref[...]
        gate = jax.nn.sigmoid(_dot(_rmsnorm_bf16(x1, gp_ref[...]), wg_ref[...]))
        y = x1 + gate * _dot(p_ref[...].astype(BF16), wp_ref[...])
        if final_norm:
            ms = jnp.mean(y * y, axis=-1, keepdims=True)
            y = y * lax.rsqrt(ms + EPS) * gf_ref[...]
        o_ref[...] = y


def mlp_ple(x, g_mlp, w_up, w_down, p, layer, g_ple, w_gate, w_proj, g_final, *, final_norm, tm=1024, tk=1024):
    rows, d = x.shape
    dff = w_up.shape[1]
    tm = _row_tile(rows, tm)
    assert dff % tk == 0
    full = lambda a: pl.BlockSpec(a.shape, lambda i, k: (0, 0))
    return pl.pallas_call(
        functools.partial(_mlp_ple_kernel, final_norm=final_norm),
        grid=(rows // tm, dff // tk),
        in_specs=[pl.BlockSpec((tm, d), lambda i, k: (i, 0)),
                  full(g_mlp),
                  pl.BlockSpec((d, tk), lambda i, k: (0, k)),
                  pl.BlockSpec((tk, d), lambda i, k: (k, 0)),
                  pl.BlockSpec((None, tm, p.shape[2]), lambda i, k: (layer, i, 0)),
                  full(g_ple), full(w_gate), full(w_proj), full(g_final)],
        out_specs=pl.BlockSpec((tm, d), lambda i, k: (i, 0)),
        out_shape=jax.ShapeDtypeStruct((rows, d), F32),
        scratch_shapes=[pltpu.VMEM((tm, d), BF16), pltpu.VMEM((tm, d), F32)],
        compiler_params=_cparams("parallel", "arbitrary"),
        name="mlp_ple",
    )(x, g_mlp, w_up, w_down, p, g_ple, w_gate, w_proj, g_final)


def _column_offsets():
    ssm_w = 32 * SSM_GROUP
    swa_q, swa_kv, dil_w = SWA_Q_HEADS * HEAD_DIM, SWA_KV_HEADS * HEAD_DIM, DIL_HEADS * HEAD_DIM
    off = {'u': 0, 'swa_q': ssm_w, 'swa_k': ssm_w + swa_q, 'swa_v': ssm_w + swa_q + swa_kv}
    off['dil'] = ssm_w + swa_q + 2 * swa_kv
    off['gates'] = off['dil'] + 3 * len(DIL_PAIRS) * dil_w
    return off


def _prompt_mixer(z, n, t, lw):
    off = _column_offsets()
    rows = n * t
    z3 = z.reshape(n, t, -1)
    dil_w = DIL_HEADS * HEAD_DIM

    ya, fin = ssm_scan(z, n, t, lw)
    groups = fin.shape[-1] // SSM_STATE
    ssm_new = jnp.stack([fin[:, 0].reshape(n, groups, SSM_STATE), fin[:, 1].reshape(n, groups, SSM_STATE)], axis=-1)

    flat = lambda pairs: [a.reshape(rows, LANES) for a in pairs]
    ob = flat(band_attn(z3, q_col=off['swa_q'], k_col=off['swa_k'], v_col=off['swa_v'],
                        q_heads=SWA_Q_HEADS, kv_heads=SWA_KV_HEADS, sinks=lw['sinks']))
    keep = min(BAND, t)
    swa_new = z3[:, t - keep:, off['swa_k']:off['swa_k'] + 2 * SWA_KV_HEADS * HEAD_DIM].reshape(
        n, keep, 2, SWA_KV_HEADS, HEAD_DIM)

    ocs, lses, dil_new = [], [], []
    for gi, (win, dil) in enumerate(DIL_PAIRS):
        assert win == BAND * dil
        c0 = off['dil'] + 3 * dil_w * gi
        o, lse = band_attn(z3, q_col=c0, k_col=c0 + dil_w, v_col=c0 + 2 * dil_w,
                           q_heads=DIL_HEADS, kv_heads=DIL_HEADS, dilation=dil, with_lse=True)
        ocs.append(flat(o))
        lses.append(flat(lse))
        keep = min(win, t)
        dil_new.append(z3[:, t - keep:, c0 + dil_w:c0 + 3 * dil_w].reshape(n, keep, 2, DIL_HEADS, HEAD_DIM))
    return ya, ob, ocs, lses, (swa_new, dil_new[0], dil_new[1], dil_new[2], ssm_new)


def _sample_mixer(z, lw, cache):
    off = _column_offsets()
    n = z.shape[0]
    dil_w = DIL_HEADS * HEAD_DIM
    ssm_w = off['swa_q']
    layer = cache['layer']

    h0 = cache['ssm'].astype(F32)
    ya, s_r, s_i = ssm_step(z[:, :ssm_w], h0[..., 0].reshape(n, -1), h0[..., 1].reshape(n, -1), lw)
    ssm_new = jnp.stack([s_r.reshape(h0.shape[:-1]), s_i.reshape(h0.shape[:-1])], axis=-1)

    def attend(q_col, kv_col, q_heads, kv_heads, buf, dilation, sinks=None):
        grp = q_heads // kv_heads
        q = z[:, q_col:q_col + q_heads * HEAD_DIM].reshape(n, kv_heads, grp, HEAD_DIM)
        new_kv = z[:, kv_col:kv_col + 2 * kv_heads * HEAD_DIM].reshape(n, 2, kv_heads, HEAD_DIM)
        if sinks is not None:
            sinks = sinks.reshape(kv_heads, grp)
        o, lse = decode_attn(q, buf, layer, new_kv, dilation=dilation, sinks=sinks)
        pairs = lambda a: [a.reshape(n, q_heads * HEAD_DIM)[:, j * LANES:(j + 1) * LANES]
                           for j in range(q_heads * HEAD_DIM // LANES)]
        return pairs(o), pairs(lse), new_kv[:, None]

    ob, _, swa_new = attend(off['swa_q'], off['swa_k'], SWA_Q_HEADS, SWA_KV_HEADS, cache['swa'], 1,
                            sinks=lw['sinks'])
    ocs, lses, dil_new = [], [], []
    for gi, (win, dil) in enumerate(DIL_PAIRS):
        c0 = off['dil'] + 3 * dil_w * gi
        o, lse, new = attend(c0, c0 + dil_w, DIL_HEADS, DIL_HEADS, cache['dil'][gi], dil)
        ocs.append(o)
        lses.append(lse)
        dil_new.append(new)
    return ya, ob, ocs, lses, (swa_new, dil_new[0], dil_new[1], dil_new[2], ssm_new)


def _layer(x, p, layer, lw, cache, n, t, g_final, final_norm):
    off = _column_offsets()
    row1 = lambda a: a.astype(F32).reshape(1, -1)
    z = norm_matmul(x, row1(lw['g_mix']), lw['w_in'][:, :off['gates']].astype(BF16))
    if cache is None:
        ya, ob, ocs, lses, st = _prompt_mixer(z, n, t, lw)
    else:
        assert t == 1
        win_rows = [c.shape[2] for c in (cache['swa'],) + tuple(cache['dil'])]
        assert win_rows == [BAND] + [w for w, _ in DIL_PAIRS], win_rows
        ya, ob, ocs, lses, st = _sample_mixer(z, lw, cache)
    x = merge(x, off['gates'], ya, ob, ocs, lses, lw)
    x = mlp_ple(x, row1(lw['g_mlp']), lw['w_up'].astype(BF16), lw['w_down'].astype(BF16), p, layer,
                row1(lw['g_ple']), lw['w_ple_gate'].astype(BF16), lw['w_ple_proj'].astype(BF16), row1(g_final),
                final_norm=final_norm)
    return x, st


def kernel(x_prompt, x_sample, cache_swa_kv, cache_dil_d1_kv, cache_dil_d4_kv, cache_dil_d16_kv, state_ssm, p_prompt, p_sample, w_in, g_mix, ssm_a_re, ssm_a_im, ssm_log_dt, ssm_b_re, ssm_b_im, ssm_c_re, ssm_c_im, ssm_d, w_glu, b_glu, attn_sinks, w_branch_a, w_branch_b, w_branch_c, w_out, g_mlp, w_up, w_down, g_ple, w_ple_gate, w_ple_proj, g_final):
    depth = w_in.shape[0]
    bp, tp, d = x_prompt.shape
    bs, ts, _ = x_sample.shape
    yp = x_prompt.reshape(bp * tp, d)
    ys = x_sample.reshape(bs * ts, d)
    st_p, st_s = [], []
    for l in range(depth):
        lw = {'w_in': w_in[l], 'g_mix': g_mix[l], 'a_re': ssm_a_re[l], 'a_im': ssm_a_im[l],
              'log_dt': ssm_log_dt[l], 'b_re': ssm_b_re[l], 'b_im': ssm_b_im[l], 'c_re': ssm_c_re[l],
              'c_im': ssm_c_im[l], 'd': ssm_d[l], 'w_glu': w_glu[l], 'b_glu': b_glu[l], 'sinks': attn_sinks[l],
              'w_branch_a': w_branch_a[l], 'w_branch_b': w_branch_b[l], 'w_branch_c': w_branch_c[l],
              'w_out': w_out[l], 'g_mlp': g_mlp[l], 'w_up': w_up[l], 'w_down': w_down[l], 'g_ple': g_ple[l],
              'w_ple_gate': w_ple_gate[l], 'w_ple_proj': w_ple_proj[l]}
        cache_l = {'layer': l, 'swa': cache_swa_kv,
                   'dil': (cache_dil_d1_kv, cache_dil_d4_kv, cache_dil_d16_kv),
                   'ssm': state_ssm[l]}
        last = l == depth - 1
        yp, sp = _layer(yp, p_prompt.reshape(depth, bp * tp, -1), l, lw, None, bp, tp, g_final, last)
        ys, ss = _layer(ys, p_sample.reshape(depth, bs * ts, -1), l, lw, cache_l, bs, ts, g_final, last)
        st_p.append(sp)
        st_s.append(ss)
    stk = lambda sts, i: jnp.stack([s[i] for s in sts])
    return (yp.reshape(bp, tp, d), ys.reshape(bs, ts, d),
            stk(st_p, 0), stk(st_p, 1), stk(st_p, 2), stk(st_p, 3), stk(st_p, 4),
            stk(st_s, 0), stk(st_s, 1), stk(st_s, 2), stk(st_s, 3), stk(st_s, 4))
```

```python
import functools

import jax
import jax.numpy as jnp
from jax import lax
from jax.experimental import pallas as pl
from jax.experimental.pallas import tpu as pltpu

HEAD_DIM = 64
SSM_GROUP = 16
SSM_STATE = 64
SWA_Q_HEADS = 8
SWA_KV_HEADS = 2
DIL_HEADS = 4
DIL_PAIRS = ((128, 1), (512, 4), (2048, 16))
BAND = 128
SSM_CHUNK = 16
SSM_TILE_GROUPS = 16
EPS = 1e-6
LANES = 128
NEG = -1e30
VMEM_LIMIT = 48 * 1024 * 1024
BF16 = jnp.bfloat16
F32 = jnp.float32


def _cparams(*sem):
    return pltpu.CompilerParams(dimension_semantics=sem, vmem_limit_bytes=VMEM_LIMIT)


def _rmsnorm_bf16(xf, g):
    ms = jnp.mean(xf * xf, axis=-1, keepdims=True)
    return (xf * lax.rsqrt(ms + EPS) * g).astype(BF16)


def _dot(a, b):
    return jnp.dot(a, b, preferred_element_type=F32)


def _row_tile(rows, pref):
    t = min(rows, pref)
    assert rows % t == 0, (rows, pref)
    return t


def _norm_matmul_kernel(x_ref, g_ref, w_ref, o_ref):
    o_ref[...] = _dot(_rmsnorm_bf16(x_ref[...], g_ref[...]), w_ref[...])


def norm_matmul(x, g, w, *, tm=512):
    rows, d = x.shape
    cols = w.shape[1]
    tm = _row_tile(rows, tm)
    return pl.pallas_call(
        _norm_matmul_kernel,
        grid=(rows // tm,),
        in_specs=[pl.BlockSpec((tm, d), lambda i: (i, 0)),
                  pl.BlockSpec((1, d), lambda i: (0, 0)),
                  pl.BlockSpec((d, cols), lambda i: (0, 0))],
        out_specs=pl.BlockSpec((tm, cols), lambda i: (i, 0)),
        out_shape=jax.ShapeDtypeStruct((rows, cols), F32),
        compiler_params=_cparams("parallel"),
        name="norm_matmul",
    )(x, g, w)


def _ssm_discretize(lw):
    a_re, a_im = lw['a_re'].astype(F32), lw['a_im'].astype(F32)
    dt = jnp.exp(lw['log_dt'].astype(F32))[:, None]
    mag = jnp.exp(a_re * dt)
    lam_r, lam_i = mag * jnp.cos(a_im * dt), mag * jnp.sin(a_im * dt)
    den = a_re * a_re + a_im * a_im
    zr = ((lam_r - 1.0) * a_re + lam_i * a_im) / den
    zi = (lam_i * a_re - (lam_r - 1.0) * a_im) / den
    b_re, b_im = lw['b_re'].astype(F32), lw['b_im'].astype(F32)
    bb_r = zr[..., None] * b_re - zi[..., None] * b_im
    bb_i = zr[..., None] * b_im + zi[..., None] * b_re
    return lam_r, lam_i, bb_r, bb_i


def _ssm_scan_operands(lw):
    lam_r, lam_i, bb_r, bb_i = _ssm_discretize(lw)
    g, p, h = bb_r.shape
    tg = SSM_TILE_GROUPS
    eye = jnp.eye(tg, dtype=F32)
    tile_in = lambda b: (b.reshape(g // tg, tg, p, h).transpose(0, 1, 3, 2)[:, :, :, None, :]
                         * eye[None, :, None, :, None]).reshape(g // tg, tg * h, tg * p)
    tile_out = lambda c: (c.reshape(g // tg, tg, h, p).transpose(0, 1, 3, 2)[:, :, :, None, :]
                          * eye[None, :, None, :, None]).reshape(g // tg, tg * p, tg * h)
    pw_r, pw_i = [lam_r], [lam_i]
    for _ in range(SSM_CHUNK - 1):
        pr, pi = pw_r[-1], pw_i[-1]
        pw_r.append(pr * lam_r - pi * lam_i)
        pw_i.append(pr * lam_i + pi * lam_r)
    pw = jnp.stack([jnp.stack(pw_r).reshape(SSM_CHUNK, g * p), jnp.stack(pw_i).reshape(SSM_CHUNK, g * p)])
    return (tile_in(bb_r).astype(BF16), tile_in(bb_i).astype(BF16),
            tile_out(lw['c_re'].astype(F32)).astype(BF16), tile_out(lw['c_im'].astype(F32)).astype(BF16),
            pw, lw['d'].astype(F32).reshape(1, g * h))


def _proj_ssm_kernel(x_ref, g_ref, w_ref, bbr_ref, bbi_ref, ccr_ref, cci_ref, pw_ref, d_ref, z_ref, y_ref, fin_ref,
                     uscr, xr, xi, cr, ci, carry, yperm, *, lane_block):
    tiles, tin, tst = bbr_ref.shape
    tb, sw = xr.shape
    n_ucols = uscr.shape[0]
    wu = n_ucols * LANES
    chunks = tb // SSM_CHUNK

    @pl.when(pl.program_id(1) == 0)
    def _():
        carry[...] = jnp.zeros_like(carry)

    hn = _rmsnorm_bf16(x_ref[...], g_ref[...])
    zu = _dot(hn, w_ref[:, :wu])
    z_ref[:, :wu] = zu
    for l in range(n_ucols):
        uscr[l] = zu[:, l * LANES:(l + 1) * LANES]
    z_ref[:, wu:] = _dot(hn, w_ref[:, wu:])

    u = jnp.concatenate(
        [jnp.concatenate([uscr.at[l][pl.ds(j, chunks, stride=SSM_CHUNK), :] for j in range(SSM_CHUNK)], axis=0)
         for l in range(n_ucols)], axis=-1)
    ub = u.astype(BF16)
    for k in range(tiles):
        xr[:, k * tst:(k + 1) * tst] = _dot(ub[:, k * tin:(k + 1) * tin], bbr_ref[k])
        xi[:, k * tst:(k + 1) * tst] = _dot(ub[:, k * tin:(k + 1) * tin], bbi_ref[k])

    step_rows = lambda j: pl.ds(j * chunks, chunks)
    for lb in range(sw // lane_block):
        cols = pl.ds(lb * lane_block, lane_block)
        lr, li = pw_ref[0, 0:1, cols], pw_ref[1, 0:1, cols]
        sr, si = xr[step_rows(0), cols], xi[step_rows(0), cols]
        for j in range(1, SSM_CHUNK):
            nr = lr * sr - li * si + xr[step_rows(j), cols]
            ni = lr * si + li * sr + xi[step_rows(j), cols]
            xr[step_rows(j), cols] = nr
            xi[step_rows(j), cols] = ni
            sr, si = nr, ni

    lcr, lci = pw_ref[0, SSM_CHUNK - 1:SSM_CHUNK, :], pw_ref[1, SSM_CHUNK - 1:SSM_CHUNK, :]

    def chunk_step(c, state):
        kr, ki = state
        cr[pl.ds(c, 1), :] = kr
        ci[pl.ds(c, 1), :] = ki
        last = (SSM_CHUNK - 1) * chunks + c
        return (lcr * kr - lci * ki + xr[pl.ds(last, 1), :], lcr * ki + lci * kr + xi[pl.ds(last, 1), :])

    kr, ki = lax.fori_loop(0, chunks, chunk_step, (carry[0:1, :], carry[1:2, :]))
    carry[0:1, :] = kr
    carry[1:2, :] = ki
    fin_ref[0:1, :] = kr
    fin_ref[1:2, :] = ki

    for lb in range(sw // lane_block):
        cols = pl.ds(lb * lane_block, lane_block)
        ckr, cki = cr[:, cols], ci[:, cols]
        for j in range(SSM_CHUNK):
            pr, pi = pw_ref[0, j:j + 1, cols], pw_ref[1, j:j + 1, cols]
            xr[step_rows(j), cols] = xr[step_rows(j), cols] + (pr * ckr - pi * cki)
            xi[step_rows(j), cols] = xi[step_rows(j), cols] + (pr * cki + pi * ckr)

    per_tile = tin // LANES
    for k in range(tiles):
        st = pl.ds(k * tst, tst)
        y = (_dot(xr[:, st].astype(BF16), ccr_ref[k]) - _dot(xi[:, st].astype(BF16), cci_ref[k])
             + u[:, k * tin:(k + 1) * tin] * d_ref[:, k * tin:(k + 1) * tin])
        for l in range(per_tile):
            yperm[k * per_tile + l] = y[:, l * LANES:(l + 1) * LANES]
    for c in range(chunks):
        for l in range(n_ucols):
            y_ref[c * SSM_CHUNK:(c + 1) * SSM_CHUNK, l * LANES:(l + 1) * LANES] = (
                yperm.at[l][pl.ds(c, SSM_CHUNK, stride=chunks), :])


def proj_ssm(x, g, w, n, t, lw, *, tb=512, lane_block=256):
    ops = _ssm_scan_operands(lw)
    bbr = ops[0]
    tiles, tin, tst = bbr.shape
    wu, sw = tiles * tin, tiles * tst
    d, cols = w.shape
    tb = _row_tile(t, tb)
    assert tb % (8 * SSM_CHUNK) == 0 and sw % lane_block == 0 and tin % LANES == 0
    nt = t // tb
    n_ucols = wu // LANES
    const = lambda a: pl.BlockSpec(a.shape, lambda i, k: (0,) * a.ndim, pipeline_mode=pl.Buffered(1))
    rows = lambda width: pl.BlockSpec((tb, width), lambda i, k: (i * nt + k, 0))
    return pl.pallas_call(
        functools.partial(_proj_ssm_kernel, lane_block=lane_block),
        grid=(n, nt),
        in_specs=[rows(d), const(g), const(w)] + [const(a) for a in ops],
        out_specs=[rows(cols), rows(wu), pl.BlockSpec((None, 2, sw), lambda i, k: (i, 0, 0))],
        out_shape=[jax.ShapeDtypeStruct((n * t, cols), F32), jax.ShapeDtypeStruct((n * t, wu), F32),
                   jax.ShapeDtypeStruct((n, 2, sw), F32)],
        scratch_shapes=[pltpu.VMEM((n_ucols, tb, LANES), F32),
                        pltpu.VMEM((tb, sw), F32), pltpu.VMEM((tb, sw), F32),
                        pltpu.VMEM((tb // SSM_CHUNK, sw), F32), pltpu.VMEM((tb // SSM_CHUNK, sw), F32),
                        pltpu.VMEM((2, sw), F32), pltpu.VMEM((n_ucols, tb, LANES), F32)],
        compiler_params=_cparams("parallel", "arbitrary"),
        name="proj_ssm",
    )(x, g, w, *ops)


def _ssm_step_kernel(u_ref, h0r_ref, h0i_ref, bbr_ref, bbi_ref, ccr_ref, cci_ref, lr_ref, li_ref, d_ref,
                     y_ref, sr_ref, si_ref):
    u = u_ref[...]
    ub = u.astype(BF16)
    lr, li = lr_ref[...], li_ref[...]
    h0r, h0i = h0r_ref[...], h0i_ref[...]
    sr = _dot(ub, bbr_ref[...]) + (lr * h0r - li * h0i)
    si = _dot(ub, bbi_ref[...]) + (lr * h0i + li * h0r)
    sr_ref[...] = sr
    si_ref[...] = si
    y_ref[...] = _dot(sr.astype(BF16), ccr_ref[...]) - _dot(si.astype(BF16), cci_ref[...]) + u * d_ref[...]


def ssm_step(u, h0r, h0i, lw):
    lam_r, lam_i, bb_r, bb_i = _ssm_discretize(lw)
    g, p, h = bb_r.shape
    eye = jnp.eye(g, dtype=F32)
    dense_in = lambda b: (b.transpose(0, 2, 1)[:, :, None, :] * eye[:, None, :, None]).reshape(g * h, g * p)
    dense_out = lambda c: (c.transpose(0, 2, 1)[:, :, None, :] * eye[:, None, :, None]).reshape(g * p, g * h)
    n = u.shape[0]
    args = (u, h0r, h0i, dense_in(bb_r).astype(BF16), dense_in(bb_i).astype(BF16),
            dense_out(lw['c_re'].astype(F32)).astype(BF16), dense_out(lw['c_im'].astype(F32)).astype(BF16),
            lam_r.reshape(1, g * p), lam_i.reshape(1, g * p), lw['d'].astype(F32).reshape(1, g * h))
    full = lambda a: pl.BlockSpec(a.shape, lambda i: (0, 0))
    return pl.pallas_call(
        _ssm_step_kernel,
        grid=(1,),
        in_specs=[full(a) for a in args],
        out_specs=[pl.BlockSpec((n, g * h), lambda i: (0, 0)),
                   pl.BlockSpec((n, g * p), lambda i: (0, 0)),
                   pl.BlockSpec((n, g * p), lambda i: (0, 0))],
        out_shape=[jax.ShapeDtypeStruct((n, g * h), F32),
                   jax.ShapeDtypeStruct((n, g * p), F32),
                   jax.ShapeDtypeStruct((n, g * p), F32)],
        compiler_params=_cparams("arbitrary"),
        name="ssm_step",
    )(*args)


def _band_attn_kernel(*refs, n_pairs, kv_pairs, heads, dilation, nblk, with_sink, with_lse):
    it = iter(refs)
    q_refs = [next(it) for _ in range(n_pairs)]
    kv_refs = [[next(it) for _ in range(4)] for _ in range(kv_pairs)]
    sink_ref = next(it) if with_sink else None
    o_refs = [next(it) for _ in range(n_pairs)]
    lse_refs = [next(it) for _ in range(n_pairs)] if with_lse else None
    b = pl.program_id(1)
    kj = lax.broadcasted_iota(jnp.int32, (2 * BAND, BAND), 0)
    qi = lax.broadcasted_iota(jnp.int32, (2 * BAND, BAND), 1)
    in_band = (kj >= qi) & (kj <= qi + BAND)
    bias_rest = jnp.where(in_band, 0.0, NEG)
    bias_first = jnp.where(in_band & ((kj >= BAND) | (b > 0)), 0.0, NEG)
    lane = lax.broadcasted_iota(jnp.int32, (1, LANES), 1)
    lane_half = [(lane < HEAD_DIM).astype(F32), (lane >= HEAD_DIM).astype(F32)]
    row = lax.broadcasted_iota(jnp.int32, (LANES, 1), 0)
    row_half = [row < HEAD_DIM, row >= HEAD_DIM]

    members = {}
    for j in range(n_pairs):
        for half in (0, 1):
            members.setdefault(heads[j][half], []).append((half, j))

    def block(r, i):
        rows = pl.ds(r, BAND, stride=dilation) if dilation > 1 else pl.ds(i * BAND, BAND)
        bias = bias_first if i == 0 else bias_rest

        def with_prev(prev_ref, cur_ref):
            if i > 0:
                prev = cur_ref[pl.ds((i - 1) * BAND, BAND), :]
            else:
                prev = prev_ref[rows, :] if dilation > 1 else prev_ref[...]
            return jnp.concatenate([prev, cur_ref[rows, :]], axis=0)

        q2 = [q_refs[j][rows, :] * (HEAD_DIM ** -0.5) for j in range(n_pairs)]
        ot = [jnp.zeros((LANES, BAND), F32) for _ in range(n_pairs)]
        lset = [jnp.zeros((LANES, BAND), F32) for _ in range(n_pairs)]
        base = {}
        for (pair, rot), group in members.items():
            group = sorted(group)
            if pair not in base:
                kp_ref, kc_ref, vp_ref, vc_ref = kv_refs[pair]
                base[pair] = (with_prev(kp_ref, kc_ref), with_prev(vp_ref, vc_ref).T)
            k2, vt = base[pair]
            if rot:
                k2 = pltpu.roll(k2, HEAD_DIM, 1)
                vt = jnp.concatenate([vt[HEAD_DIM:], vt[:HEAD_DIM]], axis=0)
            nh = len(group)
            qm = jnp.concatenate([(q2[j] * lane_half[half]).astype(BF16) for half, j in group], axis=0)
            st = lax.dot_general(k2.astype(BF16), qm, (((1,), (1,)), ((), ())), preferred_element_type=F32)
            st = st + jnp.concatenate([bias] * nh, axis=1)
            m = jnp.max(st, axis=0, keepdims=True)
            e = jnp.exp(st - m)
            lse = m + jnp.log(jnp.sum(e, axis=0, keepdims=True))
            if with_sink:
                sk = jnp.concatenate([jnp.full((1, BAND), sink_ref[2 * j + half], F32) for half, j in group], axis=1)
                mx = jnp.maximum(lse, sk)
                lse = mx + jnp.log(jnp.exp(lse - mx) + jnp.exp(sk - mx))
            pt = (e * jnp.exp(m - lse)).astype(BF16)
            for half in (0, 1):
                idx = [g for g, (hf, _) in enumerate(group) if hf == half]
                if not idx:
                    continue
                vb = jnp.where(row_half[half], vt, 0.0).astype(BF16)
                res = _dot(vb, pt[:, idx[0] * BAND:(idx[-1] + 1) * BAND])
                for n_, g in enumerate(idx):
                    j = group[g][1]
                    ot[j] = ot[j] + res[:, n_ * BAND:(n_ + 1) * BAND]
                    if with_lse:
                        lset[j] = jnp.where(row_half[half], lse[:, g * BAND:(g + 1) * BAND], lset[j])
        for j in range(n_pairs):
            o_refs[j][rows, :] = ot[j].T
            if with_lse:
                lse_refs[j][rows, :] = lset[j].T

    if dilation == 1:
        for i in range(nblk):
            block(0, i)
    else:
        def body(r, c):
            block(r, 0)
            return c
        lax.fori_loop(0, dilation, body, 0, unroll=min(dilation, nblk))


def band_attn(src, *, q_col, k_col, v_col, q_heads, kv_heads, dilation=1, sinks=None, with_lse=False, nblk=8):
    s_, t_, _ = src.shape
    blk = BAND * (dilation if dilation > 1 else nblk)
    assert t_ % blk == 0 and q_heads % 2 == 0 and kv_heads % 2 == 0
    grp = q_heads // kv_heads
    n_pairs = q_heads // 2
    kv_pairs = kv_heads // 2
    assert q_col % LANES == 0 and k_col % LANES == 0 and v_col % LANES == 0
    heads = []
    for j in range(n_pairs):
        pair_heads = []
        for half in (0, 1):
            kvh = (2 * j + half) // grp
            pair_heads.append((kvh // 2, (kvh % 2) != half))
        heads.append(tuple(pair_heads))
    cur = lambda col: pl.BlockSpec((None, blk, LANES), lambda n, b: (n, b, col // LANES))
    if dilation > 1:
        prev = lambda col: pl.BlockSpec((None, blk, LANES), lambda n, b: (n, jnp.maximum(b - 1, 0), col // LANES))
    else:
        prev = lambda col: pl.BlockSpec((None, BAND, LANES),
                                        lambda n, b: (n, jnp.maximum(b * nblk - 1, 0), col // LANES))
    in_specs = [cur(q_col + j * LANES) for j in range(n_pairs)]
    for kp in range(kv_pairs):
        kc, vc = k_col + kp * LANES, v_col + kp * LANES
        in_specs += [prev(kc), cur(kc), prev(vc), cur(vc)]
    args = [src] * len(in_specs)
    if sinks is not None:
        in_specs.append(pl.BlockSpec(memory_space=pltpu.SMEM))
        args.append(sinks.astype(F32))
    n_out = n_pairs * (2 if with_lse else 1)
    out = pl.pallas_call(
        functools.partial(_band_attn_kernel, n_pairs=n_pairs, kv_pairs=kv_pairs, heads=tuple(heads),
                          dilation=dilation, nblk=nblk, with_sink=sinks is not None, with_lse=with_lse),
        grid=(s_, t_ // blk),
        in_specs=in_specs,
        out_specs=[pl.BlockSpec((None, blk, LANES), lambda n, b: (n, b, 0))] * n_out,
        out_shape=[jax.ShapeDtypeStruct((s_, t_, LANES), F32)] * n_out,
        compiler_params=_cparams("parallel", "arbitrary"),
        name="band_attn",
    )(*args)
    return (out[:n_pairs], out[n_pairs:]) if with_lse else out


def _decode_attn_kernel(*refs, nb, hkv, dilation, with_sink):
    it = iter(refs)
    q_ref, kv_ref, new_ref = next(it), next(it), next(it)
    sink_ref = next(it) if with_sink else None
    o_ref, lse_ref = next(it), next(it)
    rows = kv_ref.shape[-1]
    gq = q_ref.shape[2]
    pos = lax.broadcasted_iota(jnp.int32, (1, rows), 1)
    bias = jnp.where((pos & (dilation - 1)) == 0, 0.0, NEG)
    scale = HEAD_DIM ** -0.5

    sb = max(d for d in (1, 2, 4) if nb % d == 0)
    heads = [(j, h) for j in range(sb) for h in range(hkv)]

    def per_group(it, c):
        i0 = it * sb
        qs = [q_ref[i0 + j, h] * scale for j, h in heads]
        new = [new_ref[i0 + j] for j in range(sb)]
        s = jnp.concatenate([_dot(q.astype(BF16), kv_ref[i0 + j, 0, h].astype(BF16))
                             for q, (j, h) in zip(qs, heads)], axis=0) + bias
        sn = jnp.concatenate([jnp.sum(q * new[j][0, h:h + 1, :], axis=-1, keepdims=True)
                              for q, (j, h) in zip(qs, heads)], axis=0)
        m = jnp.maximum(jnp.max(s, axis=-1, keepdims=True), sn)
        e, en = jnp.exp(s - m), jnp.exp(sn - m)
        lse = m + jnp.log(jnp.sum(e, axis=-1, keepdims=True) + en)
        if with_sink:
            sk = jnp.concatenate([sink_ref[...]] * sb, axis=0)
            mx = jnp.maximum(lse, sk)
            lse = mx + jnp.log(jnp.exp(lse - mx) + jnp.exp(sk - mx))
        w = jnp.exp(m - lse)
        p, pn = (e * w).astype(BF16), en * w
        for n_, (j, h) in enumerate(heads):
            hs = slice(n_ * gq, (n_ + 1) * gq)
            o = lax.dot_general(p[hs], kv_ref[i0 + j, 1, h].astype(BF16), (((1,), (1,)), ((), ())),
                                preferred_element_type=F32)
            o_ref[i0 + j, h] = o + pn[hs] * new[j][1, h:h + 1, :]
            lse_ref[i0 + j, h] = jnp.broadcast_to(lse[hs], (gq, HEAD_DIM))
        return c

    lax.fori_loop(0, nb // sb, per_group, 0)


def decode_attn(q, cache, layer, new_kv, *, dilation, sinks=None, block_bytes=8 * 1024 * 1024):
    n, hkv, grp, dh = q.shape
    rows = cache.shape[2]
    gq = -(-grp // 8) * 8
    assert dilation & (dilation - 1) == 0
    nb = max(1, min(n, block_bytes // (2 * hkv * dh * rows * 4)))
    while n % nb:
        nb -= 1
    pad_g = lambda a: jnp.pad(a, [(0, 0)] * (a.ndim - 2) + [(0, gq - grp), (0, 0)])
    cache_t = cache.transpose(0, 1, 3, 4, 5, 2)
    in_specs = [pl.BlockSpec((nb, hkv, gq, dh), lambda i: (i, 0, 0, 0)),
                pl.BlockSpec((None, nb, 2, hkv, dh, rows), lambda i: (layer, i, 0, 0, 0, 0)),
                pl.BlockSpec((nb, 2, hkv, dh), lambda i: (i, 0, 0, 0))]
    args = [pad_g(q), cache_t, new_kv]
    if sinks is not None:
        in_specs.append(pl.BlockSpec((hkv * gq, 1), lambda i: (0, 0)))
        args.append(pad_g(sinks.astype(F32)[:, :, None]).reshape(hkv * gq, 1))
    out_spec = pl.BlockSpec((nb, hkv, gq, dh), lambda i: (i, 0, 0, 0))
    out_sds = jax.ShapeDtypeStruct((n, hkv, gq, dh), F32)
    o, lse = pl.pallas_call(
        functools.partial(_decode_attn_kernel, nb=nb, hkv=hkv, dilation=dilation, with_sink=sinks is not None),
        grid=(n // nb,),
        in_specs=in_specs,
        out_specs=[out_spec, out_spec],
        out_shape=[out_sds, out_sds],
        compiler_params=_cparams("parallel"),
        name="decode_attn",
    )(*args)
    return o[:, :, :grp], lse[:, :, :grp]


def _merge_kernel(*refs, nb_pairs, nc_pairs, n_groups):
    it = iter(refs)
    x_ref, ya_ref = next(it), next(it)
    lanes = lambda k: jnp.concatenate([next(it)[...] for _ in range(k)], axis=-1)
    ob = lanes(nb_pairs)
    ocs = [lanes(nc_pairs) for _ in range(n_groups)]
    lses = [lanes(nc_pairs) for _ in range(n_groups)]
    gmix_ref, wgate_ref = next(it), next(it)
    wglu_ref, bglu_ref, wba_ref, wbb_ref, wbc_ref, wout_ref, o_ref = (next(it) for _ in range(7))
    d = x_ref.shape[1]
    hn = _rmsnorm_bf16(x_ref[...], gmix_ref[...])
    gates = [jax.nn.sigmoid(_dot(hn, wgate_ref[:, b * d:(b + 1) * d])) for b in range(3)]
    g = jax.nn.gelu(ya_ref[...])
    oa = g * jax.nn.sigmoid(_dot(g.astype(BF16), wglu_ref[...]) + bglu_ref[...])
    mx = functools.reduce(jnp.maximum, lses)
    es = [jnp.exp(l - mx) for l in lses]
    oc = sum(e * o for e, o in zip(es, ocs)) / sum(es)
    merged = (gates[0] * _dot(oa.astype(BF16), wba_ref[...])
              + gates[1] * _dot(ob.astype(BF16), wbb_ref[...])
              + gates[2] * _dot(oc.astype(BF16), wbc_ref[...]))
    o_ref[...] = x_ref[...] + _dot(merged.astype(BF16), wout_ref[...])


def merge(x, gate_col, ya, ob, ocs, lses, lw, *, tm=512):
    rows, d = x.shape
    tm = _row_tile(rows, tm)
    row = lambda w: pl.BlockSpec((tm, w), lambda i: (i, 0))
    full = lambda a: pl.BlockSpec(a.shape, lambda i: (0, 0))
    weights = (lw['g_mix'].astype(F32).reshape(1, -1), lw['w_in'][:, gate_col:gate_col + 3 * d].astype(BF16),
               lw['w_glu'].astype(BF16), lw['b_glu'].astype(F32).reshape(1, -1), lw['w_branch_a'].astype(BF16),
               lw['w_branch_b'].astype(BF16), lw['w_branch_c'].astype(BF16), lw['w_out'].astype(BF16))
    pairs = list(ob) + [a for grp in ocs for a in grp] + [a for grp in lses for a in grp]
    return pl.pallas_call(
        functools.partial(_merge_kernel, nb_pairs=len(ob), nc_pairs=len(ocs[0]), n_groups=len(ocs)),
        grid=(rows // tm,),
        in_specs=[row(d), row(ya.shape[1])] + [row(LANES)] * len(pairs) + [full(w) for w in weights],
        out_specs=row(d),
        out_shape=jax.ShapeDtypeStruct((rows, d), F32),
        compiler_params=_cparams("parallel"),
        name="merge",
    )(x, ya, *pairs, *weights)


def _mlp_ple_kernel(x_ref, g_ref, wup_ref, wdn_ref, p_ref, gp_ref, wg_ref, wp_ref, gf_ref, o_ref, hn_ref, acc_ref,
                    *, final_norm):
    k = pl.program_id(1)

    @pl.when(k == 0)
    def _():
        hn_ref[...] = _rmsnorm_bf16(x_ref[...], g_ref[...])
        acc_ref[...] = jnp.zeros_like(acc_ref)

    h = jnp.maximum(_dot(hn_ref[...], wup_ref[...]), 0.0)
    acc_ref[...] += _dot((h * h).astype(BF16), wdn_ref[...])

    @pl.when(k == pl.num_programs(1) - 1)
    def _():
        x1 = x_ref[...] + acc_ref[...]
        gate = jax.nn.sigmoid(_dot(_rmsnorm_bf16(x1, gp_ref[...]), wg_ref[...]))
        y = x1 + gate * _dot(p_ref[...].astype(BF16), wp_ref[...])
        if final_norm:
            ms = jnp.mean(y * y, axis=-1, keepdims=True)
            y = y * lax.rsqrt(ms + EPS) * gf_ref[...]
        o_ref[...] = y


def mlp_ple(x, g_mlp, w_up, w_down, p, layer, g_ple, w_gate, w_proj, g_final, *, final_norm, tm=1024, tk=1024):
    rows, d = x.shape
    dff = w_up.shape[1]
    tm = _row_tile(rows, tm)
    assert dff % tk == 0
    full = lambda a: pl.BlockSpec(a.shape, lambda i, k: (0, 0))
    return pl.pallas_call(
        functools.partial(_mlp_ple_kernel, final_norm=final_norm),
        grid=(rows // tm, dff // tk),
        in_specs=[pl.BlockSpec((tm, d), lambda i, k: (i, 0)),
                  full(g_mlp),
                  pl.BlockSpec((d, tk), lambda i, k: (0, k)),
                  pl.BlockSpec((tk, d), lambda i, k: (k, 0)),
                  pl.BlockSpec((None, tm, p.shape[2]), lambda i, k: (layer, i, 0)),
                  full(g_ple), full(w_gate), full(w_proj), full(g_final)],
        out_specs=pl.BlockSpec((tm, d), lambda i, k: (i, 0)),
        out_shape=jax.ShapeDtypeStruct((rows, d), F32),
        scratch_shapes=[pltpu.VMEM((tm, d), BF16), pltpu.VMEM((tm, d), F32)],
        compiler_params=_cparams("parallel", "arbitrary"),
        name="mlp_ple",
    )(x, g_mlp, w_up, w_down, p, g_ple, w_gate, w_proj, g_final)


def _column_offsets():
    ssm_w = 32 * SSM_GROUP
    swa_q, swa_kv, dil_w = SWA_Q_HEADS * HEAD_DIM, SWA_KV_HEADS * HEAD_DIM, DIL_HEADS * HEAD_DIM
    off = {'u': 0, 'swa_q': ssm_w, 'swa_k': ssm_w + swa_q, 'swa_v': ssm_w + swa_q + swa_kv}
    off['dil'] = ssm_w + swa_q + 2 * swa_kv
    off['gates'] = off['dil'] + 3 * len(DIL_PAIRS) * dil_w
    return off


def _prompt_mixer(z, fin, n, t, lw):
    off = _column_offsets()
    rows = n * t
    z3 = z.reshape(n, t, -1)
    dil_w = DIL_HEADS * HEAD_DIM

    groups = fin.shape[-1] // SSM_STATE
    ssm_new = jnp.stack([fin[:, 0].reshape(n, groups, SSM_STATE), fin[:, 1].reshape(n, groups, SSM_STATE)], axis=-1)

    flat = lambda pairs: [a.reshape(rows, LANES) for a in pairs]
    ob = flat(band_attn(z3, q_col=off['swa_q'], k_col=off['swa_k'], v_col=off['swa_v'],
                        q_heads=SWA_Q_HEADS, kv_heads=SWA_KV_HEADS, sinks=lw['sinks']))
    keep = min(BAND, t)
    swa_new = z3[:, t - keep:, off['swa_k']:off['swa_k'] + 2 * SWA_KV_HEADS * HEAD_DIM].reshape(
        n, keep, 2, SWA_KV_HEADS, HEAD_DIM)

    ocs, lses, dil_new = [], [], []
    for gi, (win, dil) in enumerate(DIL_PAIRS):
        assert win == BAND * dil
        c0 = off['dil'] + 3 * dil_w * gi
        o, lse = band_attn(z3, q_col=c0, k_col=c0 + dil_w, v_col=c0 + 2 * dil_w,
                           q_heads=DIL_HEADS, kv_heads=DIL_HEADS, dilation=dil, with_lse=True)
        ocs.append(flat(o))
        lses.append(flat(lse))
        keep = min(win, t)
        dil_new.append(z3[:, t - keep:, c0 + dil_w:c0 + 3 * dil_w].reshape(n, keep, 2, DIL_HEADS, HEAD_DIM))
    return ob, ocs, lses, (swa_new, dil_new[0], dil_new[1], dil_new[2], ssm_new)


def _sample_mixer(z, lw, cache):
    off = _column_offsets()
    n = z.shape[0]
    dil_w = DIL_HEADS * HEAD_DIM
    ssm_w = off['swa_q']
    layer = cache['layer']

    h0 = cache['ssm'].astype(F32)
    ya, s_r, s_i = ssm_step(z[:, :ssm_w], h0[..., 0].reshape(n, -1), h0[..., 1].reshape(n, -1), lw)
    ssm_new = jnp.stack([s_r.reshape(h0.shape[:-1]), s_i.reshape(h0.shape[:-1])], axis=-1)

    def attend(q_col, kv_col, q_heads, kv_heads, buf, dilation, sinks=None):
        grp = q_heads // kv_heads
        q = z[:, q_col:q_col + q_heads * HEAD_DIM].reshape(n, kv_heads, grp, HEAD_DIM)
        new_kv = z[:, kv_col:kv_col + 2 * kv_heads * HEAD_DIM].reshape(n, 2, kv_heads, HEAD_DIM)
        if sinks is not None:
            sinks = sinks.reshape(kv_heads, grp)
        o, lse = decode_attn(q, buf, layer, new_kv, dilation=dilation, sinks=sinks)
        pairs = lambda a: [a.reshape(n, q_heads * HEAD_DIM)[:, j * LANES:(j + 1) * LANES]
                           for j in range(q_heads * HEAD_DIM // LANES)]
        return pairs(o), pairs(lse), new_kv[:, None]

    ob, _, swa_new = attend(off['swa_q'], off['swa_k'], SWA_Q_HEADS, SWA_KV_HEADS, cache['swa'], 1,
                            sinks=lw['sinks'])
    ocs, lses, dil_new = [], [], []
    for gi, (win, dil) in enumerate(DIL_PAIRS):
        c0 = off['dil'] + 3 * dil_w * gi
        o, lse, new = attend(c0, c0 + dil_w, DIL_HEADS, DIL_HEADS, cache['dil'][gi], dil)
        ocs.append(o)
        lses.append(lse)
        dil_new.append(new)
    return ya, ob, ocs, lses, (swa_new, dil_new[0], dil_new[1], dil_new[2], ssm_new)


def _layer(x, p, layer, lw, cache, n, t, g_final, final_norm):
    off = _column_offsets()
    row1 = lambda a: a.astype(F32).reshape(1, -1)
    w_mix = lw['w_in'][:, :off['gates']].astype(BF16)
    if cache is None:
        z, ya, fin = proj_ssm(x, row1(lw['g_mix']), w_mix, n, t, lw)
        ob, ocs, lses, st = _prompt_mixer(z, fin, n, t, lw)
    else:
        assert t == 1
        win_rows = [c.shape[2] for c in (cache['swa'],) + tuple(cache['dil'])]
        assert win_rows == [BAND] + [w for w, _ in DIL_PAIRS], win_rows
        z = norm_matmul(x, row1(lw['g_mix']), w_mix)
        ya, ob, ocs, lses, st = _sample_mixer(z, lw, cache)
    x = merge(x, off['gates'], ya, ob, ocs, lses, lw)
    x = mlp_ple(x, row1(lw['g_mlp']), lw['w_up'].astype(BF16), lw['w_down'].astype(BF16), p, layer,
                row1(lw['g_ple']), lw['w_ple_gate'].astype(BF16), lw['w_ple_proj'].astype(BF16), row1(g_final),
                final_norm=final_norm)
    return x, st


def kernel(x_prompt, x_sample, cache_swa_kv, cache_dil_d1_kv, cache_dil_d4_kv, cache_dil_d16_kv, state_ssm, p_prompt, p_sample, w_in, g_mix, ssm_a_re, ssm_a_im, ssm_log_dt, ssm_b_re, ssm_b_im, ssm_c_re, ssm_c_im, ssm_d, w_glu, b_glu, attn_sinks, w_branch_a, w_branch_b, w_branch_c, w_out, g_mlp, w_up, w_down, g_ple, w_ple_gate, w_ple_proj, g_final):
    depth = w_in.shape[0]
    bp, tp, d = x_prompt.shape
    bs, ts, _ = x_sample.shape
    yp = x_prompt.reshape(bp * tp, d)
    ys = x_sample.reshape(bs * ts, d)
    st_p, st_s = [], []
    for l in range(depth):
        lw = {'w_in': w_in[l], 'g_mix': g_mix[l], 'a_re': ssm_a_re[l], 'a_im': ssm_a_im[l],
              'log_dt': ssm_log_dt[l], 'b_re': ssm_b_re[l], 'b_im': ssm_b_im[l], 'c_re': ssm_c_re[l],
              'c_im': ssm_c_im[l], 'd': ssm_d[l], 'w_glu': w_glu[l], 'b_glu': b_glu[l], 'sinks': attn_sinks[l],
              'w_branch_a': w_branch_a[l], 'w_branch_b': w_branch_b[l], 'w_branch_c': w_branch_c[l],
              'w_out': w_out[l], 'g_mlp': g_mlp[l], 'w_up': w_up[l], 'w_down': w_down[l], 'g_ple': g_ple[l],
              'w_ple_gate': w_ple_gate[l], 'w_ple_proj': w_ple_proj[l]}
        cache_l = {'layer': l, 'swa': cache_swa_kv,
                   'dil': (cache_dil_d1_kv, cache_dil_d4_kv, cache_dil_d16_kv),
                   'ssm': state_ssm[l]}
        last = l == depth - 1
        yp, sp = _layer(yp, p_prompt.reshape(depth, bp * tp, -1), l, lw, None, bp, tp, g_final, last)
        ys, ss = _layer(ys, p_sample.reshape(depth, bs * ts, -1), l, lw, cache_l, bs, ts, g_final, last)
        st_p.append(sp)
        st_s.append(ss)
    stk = lambda sts, i: jnp.stack([s[i] for s in sts])
    return (yp.reshape(bp, tp, d), ys.reshape(bs, ts, d),
            stk(st_p, 0), stk(st_p, 1), stk(st_p, 2), stk(st_p, 3), stk(st_p, 4),
            stk(st_s, 0), stk(st_s, 1), stk(st_s, 2), stk(st_s, 3), stk(st_s, 4))
```

```python
import functools

import jax
import jax.numpy as jnp
from jax import lax
from jax.experimental import pallas as pl
from jax.experimental.pallas import tpu as pltpu

HEAD_DIM = 64
SSM_GROUP = 16
SSM_STATE = 64
SWA_Q_HEADS = 8
SWA_KV_HEADS = 2
DIL_HEADS = 4
DIL_PAIRS = ((128, 1), (512, 4), (2048, 16))
BAND = 128
SSM_CHUNK = 16
SSM_TILE_GROUPS = 16
EPS = 1e-6
LANES = 128
NEG = -1e30
VMEM_LIMIT = 48 * 1024 * 1024
BF16 = jnp.bfloat16
F32 = jnp.float32


def _cparams(*sem):
    return pltpu.CompilerParams(dimension_semantics=sem, vmem_limit_bytes=VMEM_LIMIT)


def _rmsnorm_bf16(xf, g):
    ms = jnp.mean(xf * xf, axis=-1, keepdims=True)
    return (xf * lax.rsqrt(ms + EPS) * g).astype(BF16)


def _dot(a, b):
    return jnp.dot(a, b, preferred_element_type=F32)


def _row_tile(rows, pref):
    t = min(rows, pref)
    assert rows % t == 0, (rows, pref)
    return t


def _norm_matmul_kernel(x_ref, g_ref, w_ref, o_ref):
    o_ref[...] = _dot(_rmsnorm_bf16(x_ref[...], g_ref[...]), w_ref[...])


def norm_matmul(x, g, w, *, tm=512):
    rows, d = x.shape
    cols = w.shape[1]
    tm = _row_tile(rows, tm)
    return pl.pallas_call(
        _norm_matmul_kernel,
        grid=(rows // tm,),
        in_specs=[pl.BlockSpec((tm, d), lambda i: (i, 0)),
                  pl.BlockSpec((1, d), lambda i: (0, 0)),
                  pl.BlockSpec((d, cols), lambda i: (0, 0))],
        out_specs=pl.BlockSpec((tm, cols), lambda i: (i, 0)),
        out_shape=jax.ShapeDtypeStruct((rows, cols), F32),
        compiler_params=_cparams("parallel"),
        name="norm_matmul",
    )(x, g, w)


def _ssm_discretize(lw):
    a_re, a_im = lw['a_re'].astype(F32), lw['a_im'].astype(F32)
    dt = jnp.exp(lw['log_dt'].astype(F32))[:, None]
    mag = jnp.exp(a_re * dt)
    lam_r, lam_i = mag * jnp.cos(a_im * dt), mag * jnp.sin(a_im * dt)
    den = a_re * a_re + a_im * a_im
    zr = ((lam_r - 1.0) * a_re + lam_i * a_im) / den
    zi = (lam_i * a_re - (lam_r - 1.0) * a_im) / den
    b_re, b_im = lw['b_re'].astype(F32), lw['b_im'].astype(F32)
    bb_r = zr[..., None] * b_re - zi[..., None] * b_im
    bb_i = zr[..., None] * b_im + zi[..., None] * b_re
    return lam_r, lam_i, bb_r, bb_i


def _ssm_scan_operands(lw):
    lam_r, lam_i, bb_r, bb_i = _ssm_discretize(lw)
    g, p, h = bb_r.shape
    tg = SSM_TILE_GROUPS
    eye = jnp.eye(tg, dtype=F32)
    tile_in = lambda b: (b.reshape(g // tg, tg, p, h).transpose(0, 1, 3, 2)[:, :, :, None, :]
                         * eye[None, :, None, :, None]).reshape(g // tg, tg * h, tg * p)
    tile_out = lambda c: (c.reshape(g // tg, tg, h, p).transpose(0, 1, 3, 2)[:, :, :, None, :]
                          * eye[None, :, None, :, None]).reshape(g // tg, tg * p, tg * h)
    pw_r, pw_i = [lam_r], [lam_i]
    for _ in range(SSM_CHUNK - 1):
        pr, pi = pw_r[-1], pw_i[-1]
        pw_r.append(pr * lam_r - pi * lam_i)
        pw_i.append(pr * lam_i + pi * lam_r)
    pw = jnp.stack([jnp.stack(pw_r).reshape(SSM_CHUNK, g * p), jnp.stack(pw_i).reshape(SSM_CHUNK, g * p)])
    return (tile_in(bb_r).astype(BF16), tile_in(bb_i).astype(BF16),
            tile_out(lw['c_re'].astype(F32)).astype(BF16), tile_out(lw['c_im'].astype(F32)).astype(BF16),
            pw, lw['d'].astype(F32).reshape(1, g * h))


def _proj_ssm_kernel(x_ref, g_ref, w_ref, bbr_ref, bbi_ref, ccr_ref, cci_ref, pw_ref, d_ref, z_ref, y_ref, fin_ref,
                     uscr, xr, xi, cr, ci, carry, yperm, *, lane_block):
    tiles, tin, tst = bbr_ref.shape
    tb, sw = xr.shape
    n_ucols = uscr.shape[0]
    wu = n_ucols * LANES
    chunks = tb // SSM_CHUNK

    @pl.when(pl.program_id(1) == 0)
    def _():
        carry[...] = jnp.zeros_like(carry)

    hn = _rmsnorm_bf16(x_ref[...], g_ref[...])
    zu = _dot(hn, w_ref[:, :wu])
    z_ref[:, :wu] = zu
    for l in range(n_ucols):
        uscr[l] = zu[:, l * LANES:(l + 1) * LANES]
    z_ref[:, wu:] = _dot(hn, w_ref[:, wu:])

    u = jnp.concatenate(
        [jnp.concatenate([uscr.at[l][pl.ds(j, chunks, stride=SSM_CHUNK), :] for j in range(SSM_CHUNK)], axis=0)
         for l in range(n_ucols)], axis=-1)
    ub = u.astype(BF16)

    step_rows = lambda j: pl.ds(j * chunks, chunks)
    blocks_per_tile = tst // lane_block
    for lb in range(sw // lane_block):
        k, off = lb // blocks_per_tile, (lb % blocks_per_tile) * lane_block
        cols = pl.ds(lb * lane_block, lane_block)
        ubk = ub[:, k * tin:(k + 1) * tin]
        pr = _dot(ubk, bbr_ref[k, :, off:off + lane_block])
        pi = _dot(ubk, bbi_ref[k, :, off:off + lane_block])
        lr, li = pw_ref[0, 0:1, cols], pw_ref[1, 0:1, cols]
        sr, si = pr[0:chunks], pi[0:chunks]
        xr[step_rows(0), cols] = sr
        xi[step_rows(0), cols] = si
        for j in range(1, SSM_CHUNK):
            nr = lr * sr - li * si + pr[j * chunks:(j + 1) * chunks]
            ni = lr * si + li * sr + pi[j * chunks:(j + 1) * chunks]
            xr[step_rows(j), cols] = nr
            xi[step_rows(j), cols] = ni
            sr, si = nr, ni

    lcr, lci = pw_ref[0, SSM_CHUNK - 1:SSM_CHUNK, :], pw_ref[1, SSM_CHUNK - 1:SSM_CHUNK, :]

    def chunk_step(c, state):
        kr, ki = state
        cr[pl.ds(c, 1), :] = kr
        ci[pl.ds(c, 1), :] = ki
        last = (SSM_CHUNK - 1) * chunks + c
        return (lcr * kr - lci * ki + xr[pl.ds(last, 1), :], lcr * ki + lci * kr + xi[pl.ds(last, 1), :])

    kr, ki = lax.fori_loop(0, chunks, chunk_step, (carry[0:1, :], carry[1:2, :]))
    carry[0:1, :] = kr
    carry[1:2, :] = ki
    fin_ref[0:1, :] = kr
    fin_ref[1:2, :] = ki

    ys = [None] * tiles
    for lb in range(sw // lane_block):
        k, off = lb // blocks_per_tile, (lb % blocks_per_tile) * lane_block
        cols = pl.ds(lb * lane_block, lane_block)
        ckr, cki = cr[:, cols], ci[:, cols]
        sr, si = [], []
        for j in range(SSM_CHUNK):
            pr, pi = pw_ref[0, j:j + 1, cols], pw_ref[1, j:j + 1, cols]
            sr.append(xr[step_rows(j), cols] + (pr * ckr - pi * cki))
            si.append(xi[step_rows(j), cols] + (pr * cki + pi * ckr))
        part = (_dot(jnp.concatenate(sr, axis=0).astype(BF16), ccr_ref[k, off:off + lane_block, :])
                - _dot(jnp.concatenate(si, axis=0).astype(BF16), cci_ref[k, off:off + lane_block, :]))
        ys[k] = part if ys[k] is None else ys[k] + part

    per_tile = tin // LANES
    for k in range(tiles):
        y = ys[k] + u[:, k * tin:(k + 1) * tin] * d_ref[:, k * tin:(k + 1) * tin]
        for l in range(per_tile):
            yperm[k * per_tile + l] = y[:, l * LANES:(l + 1) * LANES]
    for c in range(chunks):
        for l in range(n_ucols):
            y_ref[c * SSM_CHUNK:(c + 1) * SSM_CHUNK, l * LANES:(l + 1) * LANES] = (
                yperm.at[l][pl.ds(c, SSM_CHUNK, stride=chunks), :])


def proj_ssm(x, g, w, n, t, lw, *, tb=512, lane_block=256):
    ops = _ssm_scan_operands(lw)
    bbr = ops[0]
    tiles, tin, tst = bbr.shape
    wu, sw = tiles * tin, tiles * tst
    d, cols = w.shape
    tb = _row_tile(t, tb)
    assert tb % (8 * SSM_CHUNK) == 0 and sw % lane_block == 0 and tin % LANES == 0
    nt = t // tb
    n_ucols = wu // LANES
    const = lambda a: pl.BlockSpec(a.shape, lambda i, k: (0,) * a.ndim, pipeline_mode=pl.Buffered(1))
    rows = lambda width: pl.BlockSpec((tb, width), lambda i, k: (i * nt + k, 0))
    return pl.pallas_call(
        functools.partial(_proj_ssm_kernel, lane_block=lane_block),
        grid=(n, nt),
        in_specs=[rows(d), const(g), const(w)] + [const(a) for a in ops],
        out_specs=[rows(cols), rows(wu), pl.BlockSpec((None, 2, sw), lambda i, k: (i, 0, 0))],
        out_shape=[jax.ShapeDtypeStruct((n * t, cols), F32), jax.ShapeDtypeStruct((n * t, wu), F32),
                   jax.ShapeDtypeStruct((n, 2, sw), F32)],
        scratch_shapes=[pltpu.VMEM((n_ucols, tb, LANES), F32),
                        pltpu.VMEM((tb, sw), F32), pltpu.VMEM((tb, sw), F32),
                        pltpu.VMEM((tb // SSM_CHUNK, sw), F32), pltpu.VMEM((tb // SSM_CHUNK, sw), F32),
                        pltpu.VMEM((2, sw), F32), pltpu.VMEM((n_ucols, tb, LANES), F32)],
        compiler_params=_cparams("parallel", "arbitrary"),
        name="proj_ssm",
    )(x, g, w, *ops)


def _ssm_step_kernel(u_ref, h0r_ref, h0i_ref, bbr_ref, bbi_ref, ccr_ref, cci_ref, lr_ref, li_ref, d_ref,
                     y_ref, sr_ref, si_ref):
    u = u_ref[...]
    ub = u.astype(BF16)
    lr, li = lr_ref[...], li_ref[...]
    h0r, h0i = h0r_ref[...], h0i_ref[...]
    sr = _dot(ub, bbr_ref[...]) + (lr * h0r - li * h0i)
    si = _dot(ub, bbi_ref[...]) + (lr * h0i + li * h0r)
    sr_ref[...] = sr
    si_ref[...] = si
    y_ref[...] = _dot(sr.astype(BF16), ccr_ref[...]) - _dot(si.astype(BF16), cci_ref[...]) + u * d_ref[...]


def ssm_step(u, h0r, h0i, lw):
    lam_r, lam_i, bb_r, bb_i = _ssm_discretize(lw)
    g, p, h = bb_r.shape
    eye = jnp.eye(g, dtype=F32)
    dense_in = lambda b: (b.transpose(0, 2, 1)[:, :, None, :] * eye[:, None, :, None]).reshape(g * h, g * p)
    dense_out = lambda c: (c.transpose(0, 2, 1)[:, :, None, :] * eye[:, None, :, None]).reshape(g * p, g * h)
    n = u.shape[0]
    args = (u, h0r, h0i, dense_in(bb_r).astype(BF16), dense_in(bb_i).astype(BF16),
            dense_out(lw['c_re'].astype(F32)).astype(BF16), dense_out(lw['c_im'].astype(F32)).astype(BF16),
            lam_r.reshape(1, g * p), lam_i.reshape(1, g * p), lw['d'].astype(F32).reshape(1, g * h))
    full = lambda a: pl.BlockSpec(a.shape, lambda i: (0, 0))
    return pl.pallas_call(
        _ssm_step_kernel,
        grid=(1,),
        in_specs=[full(a) for a in args],
        out_specs=[pl.BlockSpec((n, g * h), lambda i: (0, 0)),
                   pl.BlockSpec((n, g * p), lambda i: (0, 0)),
                   pl.BlockSpec((n, g * p), lambda i: (0, 0))],
        out_shape=[jax.ShapeDtypeStruct((n, g * h), F32),
                   jax.ShapeDtypeStruct((n, g * p), F32),
                   jax.ShapeDtypeStruct((n, g * p), F32)],
        compiler_params=_cparams("arbitrary"),
        name="ssm_step",
    )(*args)


def _band_attn_kernel(*refs, n_pairs, kv_pairs, heads, dilation, nblk, with_sink, with_lse):
    it = iter(refs)
    q_refs = [next(it) for _ in range(n_pairs)]
    kv_refs = [[next(it) for _ in range(4)] for _ in range(kv_pairs)]
    sink_ref = next(it) if with_sink else None
    o_refs = [next(it) for _ in range(n_pairs)]
    lse_refs = [next(it) for _ in range(n_pairs)] if with_lse else None
    b = pl.program_id(1)
    kj = lax.broadcasted_iota(jnp.int32, (2 * BAND, BAND), 0)
    qi = lax.broadcasted_iota(jnp.int32, (2 * BAND, BAND), 1)
    in_band = (kj >= qi) & (kj <= qi + BAND)
    bias_rest = jnp.where(in_band, 0.0, NEG)
    bias_first = jnp.where(in_band & ((kj >= BAND) | (b > 0)), 0.0, NEG)
    lane = lax.broadcasted_iota(jnp.int32, (1, LANES), 1)
    lane_half = [(lane < HEAD_DIM).astype(F32), (lane >= HEAD_DIM).astype(F32)]
    row = lax.broadcasted_iota(jnp.int32, (LANES, 1), 0)
    row_half = [row < HEAD_DIM, row >= HEAD_DIM]

    members = {}
    for j in range(n_pairs):
        for half in (0, 1):
            members.setdefault(heads[j][half], []).append((half, j))

    def block(r, i):
        rows = pl.ds(r, BAND, stride=dilation) if dilation > 1 else pl.ds(i * BAND, BAND)
        bias = bias_first if i == 0 else bias_rest

        def with_prev(prev_ref, cur_ref):
            if i > 0:
                prev = cur_ref[pl.ds((i - 1) * BAND, BAND), :]
            else:
                prev = prev_ref[rows, :] if dilation > 1 else prev_ref[...]
            return jnp.concatenate([prev, cur_ref[rows, :]], axis=0)

        q2 = [q_refs[j][rows, :] * (HEAD_DIM ** -0.5) for j in range(n_pairs)]
        ot = [jnp.zeros((LANES, BAND), F32) for _ in range(n_pairs)]
        lset = [jnp.zeros((LANES, BAND), F32) for _ in range(n_pairs)]
        base = {}
        for (pair, rot), group in members.items():
            group = sorted(group)
            if pair not in base:
                kp_ref, kc_ref, vp_ref, vc_ref = kv_refs[pair]
                base[pair] = (with_prev(kp_ref, kc_ref), with_prev(vp_ref, vc_ref).T)
            k2, vt = base[pair]
            if rot:
                k2 = pltpu.roll(k2, HEAD_DIM, 1)
                vt = jnp.concatenate([vt[HEAD_DIM:], vt[:HEAD_DIM]], axis=0)
            nh = len(group)
            qm = jnp.concatenate([(q2[j] * lane_half[half]).astype(BF16) for half, j in group], axis=0)
            st = lax.dot_general(k2.astype(BF16), qm, (((1,), (1,)), ((), ())), preferred_element_type=F32)
            st = st + jnp.concatenate([bias] * nh, axis=1)
            m = jnp.max(st, axis=0, keepdims=True)
            e = jnp.exp(st - m)
            lse = m + jnp.log(jnp.sum(e, axis=0, keepdims=True))
            if with_sink:
                sk = jnp.concatenate([jnp.full((1, BAND), sink_ref[2 * j + half], F32) for half, j in group], axis=1)
                mx = jnp.maximum(lse, sk)
                lse = mx + jnp.log(jnp.exp(lse - mx) + jnp.exp(sk - mx))
            pt = (e * jnp.exp(m - lse)).astype(BF16)
            for half in (0, 1):
                idx = [g for g, (hf, _) in enumerate(group) if hf == half]
                if not idx:
                    continue
                vb = jnp.where(row_half[half], vt, 0.0).astype(BF16)
                res = _dot(vb, pt[:, idx[0] * BAND:(idx[-1] + 1) * BAND])
                for n_, g in enumerate(idx):
                    j = group[g][1]
                    ot[j] = ot[j] + res[:, n_ * BAND:(n_ + 1) * BAND]
                    if with_lse:
                        lset[j] = jnp.where(row_half[half], lse[:, g * BAND:(g + 1) * BAND], lset[j])
        for j in range(n_pairs):
            o_refs[j][rows, :] = ot[j].T
            if with_lse:
                lse_refs[j][rows, :] = lset[j].T

    if dilation == 1:
        for i in range(nblk):
            block(0, i)
    else:
        def body(r, c):
            block(r, 0)
            return c
        lax.fori_loop(0, dilation, body, 0, unroll=min(dilation, nblk))


def band_attn(src, *, q_col, k_col, v_col, q_heads, kv_heads, dilation=1, sinks=None, with_lse=False, nblk=8):
    s_, t_, _ = src.shape
    blk = BAND * (dilation if dilation > 1 else nblk)
    assert t_ % blk == 0 and q_heads % 2 == 0 and kv_heads % 2 == 0
    grp = q_heads // kv_heads
    n_pairs = q_heads // 2
    kv_pairs = kv_heads // 2
    assert q_col % LANES == 0 and k_col % LANES == 0 and v_col % LANES == 0
    heads = []
    for j in range(n_pairs):
        pair_heads = []
        for half in (0, 1):
            kvh = (2 * j + half) // grp
            pair_heads.append((kvh // 2, (kvh % 2) != half))
        heads.append(tuple(pair_heads))
    cur = lambda col: pl.BlockSpec((None, blk, LANES), lambda n, b: (n, b, col // LANES))
    if dilation > 1:
        prev = lambda col: pl.BlockSpec((None, blk, LANES), lambda n, b: (n, jnp.maximum(b - 1, 0), col // LANES))
    else:
        prev = lambda col: pl.BlockSpec((None, BAND, LANES),
                                        lambda n, b: (n, jnp.maximum(b * nblk - 1, 0), col // LANES))
    in_specs = [cur(q_col + j * LANES) for j in range(n_pairs)]
    for kp in range(kv_pairs):
        kc, vc = k_col + kp * LANES, v_col + kp * LANES
        in_specs += [prev(kc), cur(kc), prev(vc), cur(vc)]
    args = [src] * len(in_specs)
    if sinks is not None:
        in_specs.append(pl.BlockSpec(memory_space=pltpu.SMEM))
        args.append(sinks.astype(F32))
    n_out = n_pairs * (2 if with_lse else 1)
    out = pl.pallas_call(
        functools.partial(_band_attn_kernel, n_pairs=n_pairs, kv_pairs=kv_pairs, heads=tuple(heads),
                          dilation=dilation, nblk=nblk, with_sink=sinks is not None, with_lse=with_lse),
        grid=(s_, t_ // blk),
        in_specs=in_specs,
        out_specs=[pl.BlockSpec((None, blk, LANES), lambda n, b: (n, b, 0))] * n_out,
        out_shape=[jax.ShapeDtypeStruct((s_, t_, LANES), F32)] * n_out,
        compiler_params=_cparams("parallel", "arbitrary"),
        name="band_attn",
    )(*args)
    return (out[:n_pairs], out[n_pairs:]) if with_lse else out


def _decode_attn_kernel(*refs, nb, hkv, dilation, with_sink):
    it = iter(refs)
    q_ref, kv_ref, new_ref = next(it), next(it), next(it)
    sink_ref = next(it) if with_sink else None
    o_ref, lse_ref = next(it), next(it)
    rows = kv_ref.shape[-1]
    gq = q_ref.shape[2]
    pos = lax.broadcasted_iota(jnp.int32, (1, rows), 1)
    bias = jnp.where((pos & (dilation - 1)) == 0, 0.0, NEG)
    scale = HEAD_DIM ** -0.5

    sb = max(d for d in (1, 2, 4) if nb % d == 0)
    heads = [(j, h) for j in range(sb) for h in range(hkv)]

    def per_group(it, c):
        i0 = it * sb
        qs = [q_ref[i0 + j, h] * scale for j, h in heads]
        new = [new_ref[i0 + j] for j in range(sb)]
        s = jnp.concatenate([_dot(q.astype(BF16), kv_ref[i0 + j, 0, h].astype(BF16))
                             for q, (j, h) in zip(qs, heads)], axis=0) + bias
        sn = jnp.concatenate([jnp.sum(q * new[j][0, h:h + 1, :], axis=-1, keepdims=True)
                              for q, (j, h) in zip(qs, heads)], axis=0)
        m = jnp.maximum(jnp.max(s, axis=-1, keepdims=True), sn)
        e, en = jnp.exp(s - m), jnp.exp(sn - m)
        lse = m + jnp.log(jnp.sum(e, axis=-1, keepdims=True) + en)
        if with_sink:
            sk = jnp.concatenate([sink_ref[...]] * sb, axis=0)
            mx = jnp.maximum(lse, sk)
            lse = mx + jnp.log(jnp.exp(lse - mx) + jnp.exp(sk - mx))
        w = jnp.exp(m - lse)
        p, pn = (e * w).astype(BF16), en * w
        for n_, (j, h) in enumerate(heads):
            hs = slice(n_ * gq, (n_ + 1) * gq)
            o = lax.dot_general(p[hs], kv_ref[i0 + j, 1, h].astype(BF16), (((1,), (1,)), ((), ())),
                                preferred_element_type=F32)
            o_ref[i0 + j, h] = o + pn[hs] * new[j][1, h:h + 1, :]
            lse_ref[i0 + j, h] = jnp.broadcast_to(lse[hs], (gq, HEAD_DIM))
        return c

    lax.fori_loop(0, nb // sb, per_group, 0)


def decode_attn(q, cache, layer, new_kv, *, dilation, sinks=None, block_bytes=8 * 1024 * 1024):
    n, hkv, grp, dh = q.shape
    rows = cache.shape[2]
    gq = -(-grp // 8) * 8
    assert dilation & (dilation - 1) == 0
    nb = max(1, min(n, block_bytes // (2 * hkv * dh * rows * 4)))
    while n % nb:
        nb -= 1
    pad_g = lambda a: jnp.pad(a, [(0, 0)] * (a.ndim - 2) + [(0, gq - grp), (0, 0)])
    cache_t = cache.transpose(0, 1, 3, 4, 5, 2)
    in_specs = [pl.BlockSpec((nb, hkv, gq, dh), lambda i: (i, 0, 0, 0)),
                pl.BlockSpec((None, nb, 2, hkv, dh, rows), lambda i: (layer, i, 0, 0, 0, 0)),
                pl.BlockSpec((nb, 2, hkv, dh), lambda i: (i, 0, 0, 0))]
    args = [pad_g(q), cache_t, new_kv]
    if sinks is not None:
        in_specs.append(pl.BlockSpec((hkv * gq, 1), lambda i: (0, 0)))
        args.append(pad_g(sinks.astype(F32)[:, :, None]).reshape(hkv * gq, 1))
    out_spec = pl.BlockSpec((nb, hkv, gq, dh), lambda i: (i, 0, 0, 0))
    out_sds = jax.ShapeDtypeStruct((n, hkv, gq, dh), F32)
    o, lse = pl.pallas_call(
        functools.partial(_decode_attn_kernel, nb=nb, hkv=hkv, dilation=dilation, with_sink=sinks is not None),
        grid=(n // nb,),
        in_specs=in_specs,
        out_specs=[out_spec, out_spec],
        out_shape=[out_sds, out_sds],
        compiler_params=_cparams("parallel"),
        name="decode_attn",
    )(*args)
    return o[:, :, :grp], lse[:, :, :grp]


def _merge_kernel(*refs, nb_pairs, nc_pairs, n_groups):
    it = iter(refs)
    x_ref, ya_ref = next(it), next(it)
    lanes = lambda k: jnp.concatenate([next(it)[...] for _ in range(k)], axis=-1)
    ob = lanes(nb_pairs)
    ocs = [lanes(nc_pairs) for _ in range(n_groups)]
    lses = [lanes(nc_pairs) for _ in range(n_groups)]
    gmix_ref, wgate_ref = next(it), next(it)
    wglu_ref, bglu_ref, wba_ref, wbb_ref, wbc_ref, wout_ref, o_ref = (next(it) for _ in range(7))
    d = x_ref.shape[1]
    hn = _rmsnorm_bf16(x_ref[...], gmix_ref[...])
    gates = [jax.nn.sigmoid(_dot(hn, wgate_ref[:, b * d:(b + 1) * d])) for b in range(3)]
    g = jax.nn.gelu(ya_ref[...])
    oa = g * jax.nn.sigmoid(_dot(g.astype(BF16), wglu_ref[...]) + bglu_ref[...])
    mx = functools.reduce(jnp.maximum, lses)
    es = [jnp.exp(l - mx) for l in lses]
    oc = sum(e * o for e, o in zip(es, ocs)) / sum(es)
    merged = (gates[0] * _dot(oa.astype(BF16), wba_ref[...])
              + gates[1] * _dot(ob.astype(BF16), wbb_ref[...])
              + gates[2] * _dot(oc.astype(BF16), wbc_ref[...]))
    o_ref[...] = x_ref[...] + _dot(merged.astype(BF16), wout_ref[...])


def merge(x, gate_col, ya, ob, ocs, lses, lw, *, tm=512):
    rows, d = x.shape
    tm = _row_tile(rows, tm)
    row = lambda w: pl.BlockSpec((tm, w), lambda i: (i, 0))
    full = lambda a: pl.BlockSpec(a.shape, lambda i: (0, 0))
    weights = (lw['g_mix'].astype(F32).reshape(1, -1), lw['w_in'][:, gate_col:gate_col + 3 * d].astype(BF16),
               lw['w_glu'].astype(BF16), lw['b_glu'].astype(F32).reshape(1, -1), lw['w_branch_a'].astype(BF16),
               lw['w_branch_b'].astype(BF16), lw['w_branch_c'].astype(BF16), lw['w_out'].astype(BF16))
    pairs = list(ob) + [a for grp in ocs for a in grp] + [a for grp in lses for a in grp]
    return pl.pallas_call(
        functools.partial(_merge_kernel, nb_pairs=len(ob), nc_pairs=len(ocs[0]), n_groups=len(ocs)),
        grid=(rows // tm,),
        in_specs=[row(d), row(ya.shape[1])] + [row(LANES)] * len(pairs) + [full(w) for w in weights],
        out_specs=row(d),
        out_shape=jax.ShapeDtypeStruct((rows, d), F32),
        compiler_params=_cparams("parallel"),
        name="merge",
    )(x, ya, *pairs, *weights)


def _mlp_ple_kernel(x_ref, g_ref, wup_ref, wdn_ref, p_ref, gp_ref, wg_ref, wp_ref, gf_ref, o_ref, *, final_norm, tk):
    x = x_ref[...]
    hn = _rmsnorm_bf16(x, g_ref[...])
    acc = None
    for k in range(wup_ref.shape[1] // tk):
        h = jnp.maximum(_dot(hn, wup_ref[:, k * tk:(k + 1) * tk]), 0.0)
        part = _dot((h * h).astype(BF16), wdn_ref[k * tk:(k + 1) * tk, :])
        acc = part if acc is None else acc + part
    x1 = x + acc
    gate = jax.nn.sigmoid(_dot(_rmsnorm_bf16(x1, gp_ref[...]), wg_ref[...]))
    y = x1 + gate * _dot(p_ref[...].astype(BF16), wp_ref[...])
    if final_norm:
        ms = jnp.mean(y * y, axis=-1, keepdims=True)
        y = y * lax.rsqrt(ms + EPS) * gf_ref[...]
    o_ref[...] = y


def mlp_ple(x, g_mlp, w_up, w_down, p, layer, g_ple, w_gate, w_proj, g_final, *, final_norm, tm=512, tk=1024):
    rows, d = x.shape
    dff = w_up.shape[1]
    tm = _row_tile(rows, tm)
    assert dff % tk == 0
    const = lambda a: pl.BlockSpec(a.shape, lambda i: (0, 0), pipeline_mode=pl.Buffered(1))
    return pl.pallas_call(
        functools.partial(_mlp_ple_kernel, final_norm=final_norm, tk=tk),
        grid=(rows // tm,),
        in_specs=[pl.BlockSpec((tm, d), lambda i: (i, 0)),
                  const(g_mlp), const(w_up), const(w_down),
                  pl.BlockSpec((None, tm, p.shape[2]), lambda i: (layer, i, 0)),
                  const(g_ple), const(w_gate), const(w_proj), const(g_final)],
        out_specs=pl.BlockSpec((tm, d), lambda i: (i, 0)),
        out_shape=jax.ShapeDtypeStruct((rows, d), F32),
        compiler_params=_cparams("parallel"),
        name="mlp_ple",
    )(x, g_mlp, w_up, w_down, p, g_ple, w_gate, w_proj, g_final)


def _column_offsets():
    ssm_w = 32 * SSM_GROUP
    swa_q, swa_kv, dil_w = SWA_Q_HEADS * HEAD_DIM, SWA_KV_HEADS * HEAD_DIM, DIL_HEADS * HEAD_DIM
    off = {'u': 0, 'swa_q': ssm_w, 'swa_k': ssm_w + swa_q, 'swa_v': ssm_w + swa_q + swa_kv}
    off['dil'] = ssm_w + swa_q + 2 * swa_kv
    off['gates'] = off['dil'] + 3 * len(DIL_PAIRS) * dil_w
    return off


def _prompt_mixer(z, fin, n, t, lw):
    off = _column_offsets()
    rows = n * t
    z3 = z.reshape(n, t, -1)
    dil_w = DIL_HEADS * HEAD_DIM

    groups = fin.shape[-1] // SSM_STATE
    ssm_new = jnp.stack([fin[:, 0].reshape(n, groups, SSM_STATE), fin[:, 1].reshape(n, groups, SSM_STATE)], axis=-1)

    flat = lambda pairs: [a.reshape(rows, LANES) for a in pairs]
    ob = flat(band_attn(z3, q_col=off['swa_q'], k_col=off['swa_k'], v_col=off['swa_v'],
                        q_heads=SWA_Q_HEADS, kv_heads=SWA_KV_HEADS, sinks=lw['sinks']))
    keep = min(BAND, t)
    swa_new = z3[:, t - keep:, off['swa_k']:off['swa_k'] + 2 * SWA_KV_HEADS * HEAD_DIM].reshape(
        n, keep, 2, SWA_KV_HEADS, HEAD_DIM)

    ocs, lses, dil_new = [], [], []
    for gi, (win, dil) in enumerate(DIL_PAIRS):
        assert win == BAND * dil
        c0 = off['dil'] + 3 * dil_w * gi
        o, lse = band_attn(z3, q_col=c0, k_col=c0 + dil_w, v_col=c0 + 2 * dil_w,
                           q_heads=DIL_HEADS, kv_heads=DIL_HEADS, dilation=dil, with_lse=True)
        ocs.append(flat(o))
        lses.append(flat(lse))
        keep = min(win, t)
        dil_new.append(z3[:, t - keep:, c0 + dil_w:c0 + 3 * dil_w].reshape(n, keep, 2, DIL_HEADS, HEAD_DIM))
    return ob, ocs, lses, (swa_new, dil_new[0], dil_new[1], dil_new[2], ssm_new)


def _sample_mixer(z, lw, cache):
    off = _column_offsets()
    n = z.shape[0]
    dil_w = DIL_HEADS * HEAD_DIM
    ssm_w = off['swa_q']
    layer = cache['layer']

    h0 = cache['ssm'].astype(F32)
    ya, s_r, s_i = ssm_step(z[:, :ssm_w], h0[..., 0].reshape(n, -1), h0[..., 1].reshape(n, -1), lw)
    ssm_new = jnp.stack([s_r.reshape(h0.shape[:-1]), s_i.reshape(h0.shape[:-1])], axis=-1)

    def attend(q_col, kv_col, q_heads, kv_heads, buf, dilation, sinks=None):
        grp = q_heads // kv_heads
        q = z[:, q_col:q_col + q_heads * HEAD_DIM].reshape(n, kv_heads, grp, HEAD_DIM)
        new_kv = z[:, kv_col:kv_col + 2 * kv_heads * HEAD_DIM].reshape(n, 2, kv_heads, HEAD_DIM)
        if sinks is not None:
            sinks = sinks.reshape(kv_heads, grp)
        o, lse = decode_attn(q, buf, layer, new_kv, dilation=dilation, sinks=sinks)
        pairs = lambda a: [a.reshape(n, q_heads * HEAD_DIM)[:, j * LANES:(j + 1) * LANES]
                           for j in range(q_heads * HEAD_DIM // LANES)]
        return pairs(o), pairs(lse), new_kv[:, None]

    ob, _, swa_new = attend(off['swa_q'], off['swa_k'], SWA_Q_HEADS, SWA_KV_HEADS, cache['swa'], 1,
                            sinks=lw['sinks'])
    ocs, lses, dil_new = [], [], []
    for gi, (win, dil) in enumerate(DIL_PAIRS):
        c0 = off['dil'] + 3 * dil_w * gi
        o, lse, new = attend(c0, c0 + dil_w, DIL_HEADS, DIL_HEADS, cache['dil'][gi], dil)
        ocs.append(o)
        lses.append(lse)
        dil_new.append(new)
    return ya, ob, ocs, lses, (swa_new, dil_new[0], dil_new[1], dil_new[2], ssm_new)


def _layer(x, p, layer, lw, cache, n, t, g_final, final_norm):
    off = _column_offsets()
    row1 = lambda a: a.astype(F32).reshape(1, -1)
    w_mix = lw['w_in'][:, :off['gates']].astype(BF16)
    if cache is None:
        z, ya, fin = proj_ssm(x, row1(lw['g_mix']), w_mix, n, t, lw)
        ob, ocs, lses, st = _prompt_mixer(z, fin, n, t, lw)
    else:
        assert t == 1
        win_rows = [c.shape[2] for c in (cache['swa'],) + tuple(cache['dil'])]
        assert win_rows == [BAND] + [w for w, _ in DIL_PAIRS], win_rows
        z = norm_matmul(x, row1(lw['g_mix']), w_mix)
        ya, ob, ocs, lses, st = _sample_mixer(z, lw, cache)
    x = merge(x, off['gates'], ya, ob, ocs, lses, lw)
    x = mlp_ple(x, row1(lw['g_mlp']), lw['w_up'].astype(BF16), lw['w_down'].astype(BF16), p, layer,
                row1(lw['g_ple']), lw['w_ple_gate'].astype(BF16), lw['w_ple_proj'].astype(BF16), row1(g_final),
                final_norm=final_norm)
    return x, st


def kernel(x_prompt, x_sample, cache_swa_kv, cache_dil_d1_kv, cache_dil_d4_kv, cache_dil_d16_kv, state_ssm, p_prompt, p_sample, w_in, g_mix, ssm_a_re, ssm_a_im, ssm_log_dt, ssm_b_re, ssm_b_im, ssm_c_re, ssm_c_im, ssm_d, w_glu, b_glu, attn_sinks, w_branch_a, w_branch_b, w_branch_c, w_out, g_mlp, w_up, w_down, g_ple, w_ple_gate, w_ple_proj, g_final):
    depth = w_in.shape[0]
    bp, tp, d = x_prompt.shape
    bs, ts, _ = x_sample.shape
    yp = x_prompt.reshape(bp * tp, d)
    ys = x_sample.reshape(bs * ts, d)
    st_p, st_s = [], []
    for l in range(depth):
        lw = {'w_in': w_in[l], 'g_mix': g_mix[l], 'a_re': ssm_a_re[l], 'a_im': ssm_a_im[l],
              'log_dt': ssm_log_dt[l], 'b_re': ssm_b_re[l], 'b_im': ssm_b_im[l], 'c_re': ssm_c_re[l],
              'c_im': ssm_c_im[l], 'd': ssm_d[l], 'w_glu': w_glu[l], 'b_glu': b_glu[l], 'sinks': attn_sinks[l],
              'w_branch_a': w_branch_a[l], 'w_branch_b': w_branch_b[l], 'w_branch_c': w_branch_c[l],
              'w_out': w_out[l], 'g_mlp': g_mlp[l], 'w_up': w_up[l], 'w_down': w_down[l], 'g_ple': g_ple[l],
              'w_ple_gate': w_ple_gate[l], 'w_ple_proj': w_ple_proj[l]}
        cache_l = {'layer': l, 'swa': cache_swa_kv,
                   'dil': (cache_dil_d1_kv, cache_dil_d4_kv, cache_dil_d16_kv),
                   'ssm': state_ssm[l]}
        last = l == depth - 1
        yp, sp = _layer(yp, p_prompt.reshape(depth, bp * tp, -1), l, lw, None, bp, tp, g_final, last)
        ys, ss = _layer(ys, p_sample.reshape(depth, bs * ts, -1), l, lw, cache_l, bs, ts, g_final, last)
        st_p.append(sp)
        st_s.append(ss)
    stk = lambda sts, i: jnp.stack([s[i] for s in sts])
    return (yp.reshape(bp, tp, d), ys.reshape(bs, ts, d),
            stk(st_p, 0), stk(st_p, 1), stk(st_p, 2), stk(st_p, 3), stk(st_p, 4),
            stk(st_s, 0), stk(st_s, 1), stk(st_s, 2), stk(st_s, 3), stk(st_s, 4))
```

```python
import functools

import jax
import jax.numpy as jnp
from jax import lax
from jax.experimental import pallas as pl
from jax.experimental.pallas import tpu as pltpu

HEAD_DIM = 64
SSM_GROUP = 16
SSM_STATE = 64
SWA_Q_HEADS = 8
SWA_KV_HEADS = 2
DIL_HEADS = 4
DIL_PAIRS = ((128, 1), (512, 4), (2048, 16))
BAND = 128
SSM_CHUNK = 16
SSM_TILE_GROUPS = 16
EPS = 1e-6
LANES = 128
NEG = -1e30
VMEM_LIMIT = 48 * 1024 * 1024
BF16 = jnp.bfloat16
F32 = jnp.float32


def _cparams(*sem):
    return pltpu.CompilerParams(dimension_semantics=sem, vmem_limit_bytes=VMEM_LIMIT)


def _rmsnorm_bf16(xf, g):
    ms = jnp.mean(xf * xf, axis=-1, keepdims=True)
    return (xf * lax.rsqrt(ms + EPS) * g).astype(BF16)


def _dot(a, b):
    return jnp.dot(a, b, preferred_element_type=F32)


def _row_tile(rows, pref):
    t = min(rows, pref)
    assert rows % t == 0, (rows, pref)
    return t


def _norm_matmul_kernel(x_ref, g_ref, w_ref, o_ref):
    o_ref[...] = _dot(_rmsnorm_bf16(x_ref[...], g_ref[...]), w_ref[...])


def norm_matmul(x, g, w, *, tm=512):
    rows, d = x.shape
    cols = w.shape[1]
    tm = _row_tile(rows, tm)
    return pl.pallas_call(
        _norm_matmul_kernel,
        grid=(rows // tm,),
        in_specs=[pl.BlockSpec((tm, d), lambda i: (i, 0)),
                  pl.BlockSpec((1, d), lambda i: (0, 0)),
                  pl.BlockSpec((d, cols), lambda i: (0, 0))],
        out_specs=pl.BlockSpec((tm, cols), lambda i: (i, 0)),
        out_shape=jax.ShapeDtypeStruct((rows, cols), F32),
        compiler_params=_cparams("parallel"),
        name="norm_matmul",
    )(x, g, w)


def _ssm_discretize(lw):
    a_re, a_im = lw['a_re'].astype(F32), lw['a_im'].astype(F32)
    dt = jnp.exp(lw['log_dt'].astype(F32))[:, None]
    mag = jnp.exp(a_re * dt)
    lam_r, lam_i = mag * jnp.cos(a_im * dt), mag * jnp.sin(a_im * dt)
    den = a_re * a_re + a_im * a_im
    zr = ((lam_r - 1.0) * a_re + lam_i * a_im) / den
    zi = (lam_i * a_re - (lam_r - 1.0) * a_im) / den
    b_re, b_im = lw['b_re'].astype(F32), lw['b_im'].astype(F32)
    bb_r = zr[..., None] * b_re - zi[..., None] * b_im
    bb_i = zr[..., None] * b_im + zi[..., None] * b_re
    return lam_r, lam_i, bb_r, bb_i


def _ssm_scan_operands(lw):
    lam_r, lam_i, bb_r, bb_i = _ssm_discretize(lw)
    g, p, h = bb_r.shape
    tg = SSM_TILE_GROUPS
    eye = jnp.eye(tg, dtype=F32)
    tile_in = lambda b: (b.reshape(g // tg, tg, p, h).transpose(0, 1, 3, 2)[:, :, :, None, :]
                         * eye[None, :, None, :, None]).reshape(g // tg, tg * h, tg * p)
    tile_out = lambda c: (c.reshape(g // tg, tg, h, p).transpose(0, 1, 3, 2)[:, :, :, None, :]
                          * eye[None, :, None, :, None]).reshape(g // tg, tg * p, tg * h)
    pw_r, pw_i = [lam_r], [lam_i]
    for _ in range(SSM_CHUNK - 1):
        pr, pi = pw_r[-1], pw_i[-1]
        pw_r.append(pr * lam_r - pi * lam_i)
        pw_i.append(pr * lam_i + pi * lam_r)
    pw = jnp.stack([jnp.stack(pw_r).reshape(SSM_CHUNK, g * p), jnp.stack(pw_i).reshape(SSM_CHUNK, g * p)])
    return (tile_in(bb_r).astype(BF16), tile_in(bb_i).astype(BF16),
            tile_out(lw['c_re'].astype(F32)).astype(BF16), tile_out(lw['c_im'].astype(F32)).astype(BF16),
            pw, lw['d'].astype(F32).reshape(1, g * h))


def _proj_ssm_kernel(x_ref, g_ref, w_ref, bbr_ref, bbi_ref, ccr_ref, cci_ref, pw_ref, d_ref, z_ref, y_ref, fin_ref,
                     uscr, xr, xi, cr, ci, carry, yperm, *, lane_block):
    tiles, tin, tst = bbr_ref.shape
    tb, sw = xr.shape
    n_ucols = uscr.shape[0]
    wu = n_ucols * LANES
    chunks = tb // SSM_CHUNK

    @pl.when(pl.program_id(1) == 0)
    def _():
        carry[...] = jnp.zeros_like(carry)

    hn = _rmsnorm_bf16(x_ref[...], g_ref[...])
    zu = _dot(hn, w_ref[:, :wu])
    z_ref[:, :wu] = zu
    for l in range(n_ucols):
        uscr[l] = zu[:, l * LANES:(l + 1) * LANES]
    z_ref[:, wu:] = _dot(hn, w_ref[:, wu:])

    u = jnp.concatenate(
        [jnp.concatenate([uscr.at[l][pl.ds(j, chunks, stride=SSM_CHUNK), :] for j in range(SSM_CHUNK)], axis=0)
         for l in range(n_ucols)], axis=-1)
    ub = u.astype(BF16)

    step_rows = lambda j: pl.ds(j * chunks, chunks)
    blocks_per_tile = tst // lane_block
    for lb in range(sw // lane_block):
        k, off = lb // blocks_per_tile, (lb % blocks_per_tile) * lane_block
        cols = pl.ds(lb * lane_block, lane_block)
        ubk = ub[:, k * tin:(k + 1) * tin]
        pr = _dot(ubk, bbr_ref[k, :, off:off + lane_block])
        pi = _dot(ubk, bbi_ref[k, :, off:off + lane_block])
        lr, li = pw_ref[0, 0:1, cols], pw_ref[1, 0:1, cols]
        sr, si = pr[0:chunks], pi[0:chunks]
        xr[step_rows(0), cols] = sr
        xi[step_rows(0), cols] = si
        for j in range(1, SSM_CHUNK):
            nr = lr * sr - li * si + pr[j * chunks:(j + 1) * chunks]
            ni = lr * si + li * sr + pi[j * chunks:(j + 1) * chunks]
            xr[step_rows(j), cols] = nr
            xi[step_rows(j), cols] = ni
            sr, si = nr, ni

    lcr, lci = pw_ref[0, SSM_CHUNK - 1:SSM_CHUNK, :], pw_ref[1, SSM_CHUNK - 1:SSM_CHUNK, :]

    def chunk_step(c, state):
        kr, ki = state
        cr[pl.ds(c, 1), :] = kr
        ci[pl.ds(c, 1), :] = ki
        last = (SSM_CHUNK - 1) * chunks + c
        return (lcr * kr - lci * ki + xr[pl.ds(last, 1), :], lcr * ki + lci * kr + xi[pl.ds(last, 1), :])

    kr, ki = lax.fori_loop(0, chunks, chunk_step, (carry[0:1, :], carry[1:2, :]))
    carry[0:1, :] = kr
    carry[1:2, :] = ki
    fin_ref[0:1, :] = kr
    fin_ref[1:2, :] = ki

    ys = [None] * tiles
    for lb in range(sw // lane_block):
        k, off = lb // blocks_per_tile, (lb % blocks_per_tile) * lane_block
        cols = pl.ds(lb * lane_block, lane_block)
        ckr, cki = cr[:, cols], ci[:, cols]
        sr, si = [], []
        for j in range(SSM_CHUNK):
            pr, pi = pw_ref[0, j:j + 1, cols], pw_ref[1, j:j + 1, cols]
            sr.append(xr[step_rows(j), cols] + (pr * ckr - pi * cki))
            si.append(xi[step_rows(j), cols] + (pr * cki + pi * ckr))
        part = (_dot(jnp.concatenate(sr, axis=0).astype(BF16), ccr_ref[k, off:off + lane_block, :])
                - _dot(jnp.concatenate(si, axis=0).astype(BF16), cci_ref[k, off:off + lane_block, :]))
        ys[k] = part if ys[k] is None else ys[k] + part

    per_tile = tin // LANES
    for k in range(tiles):
        y = ys[k] + u[:, k * tin:(k + 1) * tin] * d_ref[:, k * tin:(k + 1) * tin]
        for l in range(per_tile):
            yperm[k * per_tile + l] = y[:, l * LANES:(l + 1) * LANES]
    for c in range(chunks):
        for l in range(n_ucols):
            y_ref[c * SSM_CHUNK:(c + 1) * SSM_CHUNK, l * LANES:(l + 1) * LANES] = (
                yperm.at[l][pl.ds(c, SSM_CHUNK, stride=chunks), :])


def proj_ssm(x, g, w, n, t, lw, *, tb=512, lane_block=256):
    ops = _ssm_scan_operands(lw)
    bbr = ops[0]
    tiles, tin, tst = bbr.shape
    wu, sw = tiles * tin, tiles * tst
    d, cols = w.shape
    tb = _row_tile(t, tb)
    assert tb % (8 * SSM_CHUNK) == 0 and sw % lane_block == 0 and tin % LANES == 0
    nt = t // tb
    n_ucols = wu // LANES
    const = lambda a: pl.BlockSpec(a.shape, lambda i, k: (0,) * a.ndim, pipeline_mode=pl.Buffered(1))
    rows = lambda width: pl.BlockSpec((tb, width), lambda i, k: (i * nt + k, 0))
    return pl.pallas_call(
        functools.partial(_proj_ssm_kernel, lane_block=lane_block),
        grid=(n, nt),
        in_specs=[rows(d), const(g), const(w)] + [const(a) for a in ops],
        out_specs=[rows(cols), rows(wu), pl.BlockSpec((None, 2, sw), lambda i, k: (i, 0, 0))],
        out_shape=[jax.ShapeDtypeStruct((n * t, cols), F32), jax.ShapeDtypeStruct((n * t, wu), F32),
                   jax.ShapeDtypeStruct((n, 2, sw), F32)],
        scratch_shapes=[pltpu.VMEM((n_ucols, tb, LANES), F32),
                        pltpu.VMEM((tb, sw), F32), pltpu.VMEM((tb, sw), F32),
                        pltpu.VMEM((tb // SSM_CHUNK, sw), F32), pltpu.VMEM((tb // SSM_CHUNK, sw), F32),
                        pltpu.VMEM((2, sw), F32), pltpu.VMEM((n_ucols, tb, LANES), F32)],
        compiler_params=_cparams("parallel", "arbitrary"),
        name="proj_ssm",
    )(x, g, w, *ops)


def _ssm_step_kernel(u_ref, h0r_ref, h0i_ref, bbr_ref, bbi_ref, ccr_ref, cci_ref, lr_ref, li_ref, d_ref,
                     y_ref, sr_ref, si_ref):
    u = u_ref[...]
    ub = u.astype(BF16)
    lr, li = lr_ref[...], li_ref[...]
    h0r, h0i = h0r_ref[...], h0i_ref[...]
    sr = _dot(ub, bbr_ref[...]) + (lr * h0r - li * h0i)
    si = _dot(ub, bbi_ref[...]) + (lr * h0i + li * h0r)
    sr_ref[...] = sr
    si_ref[...] = si
    y_ref[...] = _dot(sr.astype(BF16), ccr_ref[...]) - _dot(si.astype(BF16), cci_ref[...]) + u * d_ref[...]


def ssm_step(u, h0r, h0i, lw):
    lam_r, lam_i, bb_r, bb_i = _ssm_discretize(lw)
    g, p, h = bb_r.shape
    eye = jnp.eye(g, dtype=F32)
    dense_in = lambda b: (b.transpose(0, 2, 1)[:, :, None, :] * eye[:, None, :, None]).reshape(g * h, g * p)
    dense_out = lambda c: (c.transpose(0, 2, 1)[:, :, None, :] * eye[:, None, :, None]).reshape(g * p, g * h)
    n = u.shape[0]
    args = (u, h0r, h0i, dense_in(bb_r).astype(BF16), dense_in(bb_i).astype(BF16),
            dense_out(lw['c_re'].astype(F32)).astype(BF16), dense_out(lw['c_im'].astype(F32)).astype(BF16),
            lam_r.reshape(1, g * p), lam_i.reshape(1, g * p), lw['d'].astype(F32).reshape(1, g * h))
    full = lambda a: pl.BlockSpec(a.shape, lambda i: (0, 0))
    return pl.pallas_call(
        _ssm_step_kernel,
        grid=(1,),
        in_specs=[full(a) for a in args],
        out_specs=[pl.BlockSpec((n, g * h), lambda i: (0, 0)),
                   pl.BlockSpec((n, g * p), lambda i: (0, 0)),
                   pl.BlockSpec((n, g * p), lambda i: (0, 0))],
        out_shape=[jax.ShapeDtypeStruct((n, g * h), F32),
                   jax.ShapeDtypeStruct((n, g * p), F32),
                   jax.ShapeDtypeStruct((n, g * p), F32)],
        compiler_params=_cparams("arbitrary"),
        name="ssm_step",
    )(*args)


def _band_attn_kernel(*refs, n_pairs, kv_pairs, heads, dilation, nblk, with_sink, with_lse):
    it = iter(refs)
    q_refs = [next(it) for _ in range(n_pairs)]
    kv_refs = [[next(it) for _ in range(4)] for _ in range(kv_pairs)]
    sink_ref = next(it) if with_sink else None
    o_refs = [next(it) for _ in range(n_pairs)]
    lse_refs = [next(it) for _ in range(n_pairs)] if with_lse else None
    b = pl.program_id(1)
    kj = lax.broadcasted_iota(jnp.int32, (2 * BAND, BAND), 0)
    qi = lax.broadcasted_iota(jnp.int32, (2 * BAND, BAND), 1)
    in_band = (kj >= qi) & (kj <= qi + BAND)
    bias_rest = jnp.where(in_band, 0.0, NEG)
    bias_first = jnp.where(in_band & ((kj >= BAND) | (b > 0)), 0.0, NEG)
    lane = lax.broadcasted_iota(jnp.int32, (1, LANES), 1)
    lane_half = [(lane < HEAD_DIM).astype(F32), (lane >= HEAD_DIM).astype(F32)]
    row = lax.broadcasted_iota(jnp.int32, (LANES, 1), 0)
    row_half = [row < HEAD_DIM, row >= HEAD_DIM]

    members = {}
    for j in range(n_pairs):
        for half in (0, 1):
            members.setdefault(heads[j][half], []).append((half, j))

    def block(r, i):
        rows = pl.ds(r, BAND, stride=dilation) if dilation > 1 else pl.ds(i * BAND, BAND)
        bias = bias_first if i == 0 else bias_rest

        def with_prev(prev_ref, cur_ref):
            if i > 0:
                prev = cur_ref[pl.ds((i - 1) * BAND, BAND), :]
            else:
                prev = prev_ref[rows, :] if dilation > 1 else prev_ref[...]
            return jnp.concatenate([prev, cur_ref[rows, :]], axis=0)

        q2 = [q_refs[j][rows, :] * (HEAD_DIM ** -0.5) for j in range(n_pairs)]
        ot = [jnp.zeros((LANES, BAND), F32) for _ in range(n_pairs)]
        lset = [jnp.zeros((LANES, BAND), F32) for _ in range(n_pairs)]
        base = {}
        for (pair, rot), group in members.items():
            group = sorted(group)
            if pair not in base:
                kp_ref, kc_ref, vp_ref, vc_ref = kv_refs[pair]
                base[pair] = (with_prev(kp_ref, kc_ref), with_prev(vp_ref, vc_ref).T)
            k2, vt = base[pair]
            if rot:
                k2 = pltpu.roll(k2, HEAD_DIM, 1)
                vt = jnp.concatenate([vt[HEAD_DIM:], vt[:HEAD_DIM]], axis=0)
            nh = len(group)
            qm = jnp.concatenate([(q2[j] * lane_half[half]).astype(BF16) for half, j in group], axis=0)
            st = lax.dot_general(k2.astype(BF16), qm, (((1,), (1,)), ((), ())), preferred_element_type=F32)
            st = st + jnp.concatenate([bias] * nh, axis=1)
            m = jnp.max(st, axis=0, keepdims=True)
            e = jnp.exp(st - m)
            lse = m + jnp.log(jnp.sum(e, axis=0, keepdims=True))
            if with_sink:
                sk = jnp.concatenate([jnp.full((1, BAND), sink_ref[2 * j + half], F32) for half, j in group], axis=1)
                mx = jnp.maximum(lse, sk)
                lse = mx + jnp.log(jnp.exp(lse - mx) + jnp.exp(sk - mx))
            pt = (e * jnp.exp(m - lse)).astype(BF16)
            for half in (0, 1):
                idx = [g for g, (hf, _) in enumerate(group) if hf == half]
                if not idx:
                    continue
                vb = jnp.where(row_half[half], vt, 0.0).astype(BF16)
                res = _dot(vb, pt[:, idx[0] * BAND:(idx[-1] + 1) * BAND])
                for n_, g in enumerate(idx):
                    j = group[g][1]
                    ot[j] = ot[j] + res[:, n_ * BAND:(n_ + 1) * BAND]
                    if with_lse:
                        lset[j] = jnp.where(row_half[half], lse[:, g * BAND:(g + 1) * BAND], lset[j])
        for j in range(n_pairs):
            o_refs[j][rows, :] = ot[j].T
            if with_lse:
                lse_refs[j][rows, :] = lset[j].T

    if dilation == 1:
        for i in range(nblk):
            block(0, i)
    else:
        def body(r, c):
            block(r, 0)
            return c
        lax.fori_loop(0, dilation, body, 0, unroll=min(dilation, nblk))


def band_attn(src, *, q_col, k_col, v_col, q_heads, kv_heads, dilation=1, sinks=None, with_lse=False, nblk=16):
    s_, t_, _ = src.shape
    blk = BAND * (dilation if dilation > 1 else nblk)
    assert t_ % blk == 0 and q_heads % 2 == 0 and kv_heads % 2 == 0
    grp = q_heads // kv_heads
    n_pairs = q_heads // 2
    kv_pairs = kv_heads // 2
    assert q_col % LANES == 0 and k_col % LANES == 0 and v_col % LANES == 0
    heads = []
    for j in range(n_pairs):
        pair_heads = []
        for half in (0, 1):
            kvh = (2 * j + half) // grp
            pair_heads.append((kvh // 2, (kvh % 2) != half))
        heads.append(tuple(pair_heads))
    cur = lambda col: pl.BlockSpec((None, blk, LANES), lambda n, b: (n, b, col // LANES))
    if dilation > 1:
        prev = lambda col: pl.BlockSpec((None, blk, LANES), lambda n, b: (n, jnp.maximum(b - 1, 0), col // LANES))
    else:
        prev = lambda col: pl.BlockSpec((None, BAND, LANES),
                                        lambda n, b: (n, jnp.maximum(b * nblk - 1, 0), col // LANES))
    in_specs = [cur(q_col + j * LANES) for j in range(n_pairs)]
    for kp in range(kv_pairs):
        kc, vc = k_col + kp * LANES, v_col + kp * LANES
        in_specs += [prev(kc), cur(kc), prev(vc), cur(vc)]
    args = [src] * len(in_specs)
    if sinks is not None:
        in_specs.append(pl.BlockSpec(memory_space=pltpu.SMEM))
        args.append(sinks.astype(F32))
    n_out = n_pairs * (2 if with_lse else 1)
    out = pl.pallas_call(
        functools.partial(_band_attn_kernel, n_pairs=n_pairs, kv_pairs=kv_pairs, heads=tuple(heads),
                          dilation=dilation, nblk=nblk, with_sink=sinks is not None, with_lse=with_lse),
        grid=(s_, t_ // blk),
        in_specs=in_specs,
        out_specs=[pl.BlockSpec((None, blk, LANES), lambda n, b: (n, b, 0))] * n_out,
        out_shape=[jax.ShapeDtypeStruct((s_, t_, LANES), F32)] * n_out,
        compiler_params=_cparams("parallel", "arbitrary"),
        name="band_attn",
    )(*args)
    return (out[:n_pairs], out[n_pairs:]) if with_lse else out


def _decode_attn_kernel(*refs, nb, hkv, dilation, with_sink):
    it = iter(refs)
    q_ref, kv_ref, new_ref = next(it), next(it), next(it)
    sink_ref = next(it) if with_sink else None
    o_ref, lse_ref = next(it), next(it)
    rows = kv_ref.shape[-1]
    gq = q_ref.shape[2]
    pos = lax.broadcasted_iota(jnp.int32, (1, rows), 1)
    bias = jnp.where((pos & (dilation - 1)) == 0, 0.0, NEG)
    scale = HEAD_DIM ** -0.5

    sb = max(d for d in (1, 2, 4) if nb % d == 0)
    heads = [(j, h) for j in range(sb) for h in range(hkv)]

    def per_group(it, c):
        i0 = it * sb
        qs = [q_ref[i0 + j, h] * scale for j, h in heads]
        new = [new_ref[i0 + j] for j in range(sb)]
        s = jnp.concatenate([_dot(q.astype(BF16), kv_ref[i0 + j, 0, h].astype(BF16))
                             for q, (j, h) in zip(qs, heads)], axis=0) + bias
        sn = jnp.concatenate([jnp.sum(q * new[j][0, h:h + 1, :], axis=-1, keepdims=True)
                              for q, (j, h) in zip(qs, heads)], axis=0)
        m = jnp.maximum(jnp.max(s, axis=-1, keepdims=True), sn)
        e, en = jnp.exp(s - m), jnp.exp(sn - m)
        lse = m + jnp.log(jnp.sum(e, axis=-1, keepdims=True) + en)
        if with_sink:
            sk = jnp.concatenate([sink_ref[...]] * sb, axis=0)
            mx = jnp.maximum(lse, sk)
            lse = mx + jnp.log(jnp.exp(lse - mx) + jnp.exp(sk - mx))
        w = jnp.exp(m - lse)
        p, pn = (e * w).astype(BF16), en * w
        for n_, (j, h) in enumerate(heads):
            hs = slice(n_ * gq, (n_ + 1) * gq)
            o = lax.dot_general(p[hs], kv_ref[i0 + j, 1, h].astype(BF16), (((1,), (1,)), ((), ())),
                                preferred_element_type=F32)
            o_ref[i0 + j, h] = o + pn[hs] * new[j][1, h:h + 1, :]
            lse_ref[i0 + j, h] = jnp.broadcast_to(lse[hs], (gq, HEAD_DIM))
        return c

    lax.fori_loop(0, nb // sb, per_group, 0)


def decode_attn(q, cache, layer, new_kv, *, dilation, sinks=None, block_bytes=8 * 1024 * 1024):
    n, hkv, grp, dh = q.shape
    rows = cache.shape[2]
    gq = -(-grp // 8) * 8
    assert dilation & (dilation - 1) == 0
    nb = max(1, min(n, block_bytes // (2 * hkv * dh * rows * 4)))
    while n % nb:
        nb -= 1
    pad_g = lambda a: jnp.pad(a, [(0, 0)] * (a.ndim - 2) + [(0, gq - grp), (0, 0)])
    cache_t = cache.transpose(0, 1, 3, 4, 5, 2)
    in_specs = [pl.BlockSpec((nb, hkv, gq, dh), lambda i: (i, 0, 0, 0)),
                pl.BlockSpec((None, nb, 2, hkv, dh, rows), lambda i: (layer, i, 0, 0, 0, 0)),
                pl.BlockSpec((nb, 2, hkv, dh), lambda i: (i, 0, 0, 0))]
    args = [pad_g(q), cache_t, new_kv]
    if sinks is not None:
        in_specs.append(pl.BlockSpec((hkv * gq, 1), lambda i: (0, 0)))
        args.append(pad_g(sinks.astype(F32)[:, :, None]).reshape(hkv * gq, 1))
    out_spec = pl.BlockSpec((nb, hkv, gq, dh), lambda i: (i, 0, 0, 0))
    out_sds = jax.ShapeDtypeStruct((n, hkv, gq, dh), F32)
    o, lse = pl.pallas_call(
        functools.partial(_decode_attn_kernel, nb=nb, hkv=hkv, dilation=dilation, with_sink=sinks is not None),
        grid=(n // nb,),
        in_specs=in_specs,
        out_specs=[out_spec, out_spec],
        out_shape=[out_sds, out_sds],
        compiler_params=_cparams("parallel"),
        name="decode_attn",
    )(*args)
    return o[:, :, :grp], lse[:, :, :grp]


def _merge_kernel(*refs, nb_pairs, nc_pairs, n_groups):
    it = iter(refs)
    x_ref, ya_ref = next(it), next(it)
    lanes = lambda k: jnp.concatenate([next(it)[...] for _ in range(k)], axis=-1)
    ob = lanes(nb_pairs)
    ocs = [lanes(nc_pairs) for _ in range(n_groups)]
    lses = [lanes(nc_pairs) for _ in range(n_groups)]
    gmix_ref, wgate_ref = next(it), next(it)
    wglu_ref, bglu_ref, wba_ref, wbb_ref, wbc_ref, wout_ref, o_ref = (next(it) for _ in range(7))
    d = x_ref.shape[1]
    hn = _rmsnorm_bf16(x_ref[...], gmix_ref[...])
    gates = [jax.nn.sigmoid(_dot(hn, wgate_ref[:, b * d:(b + 1) * d])) for b in range(3)]
    g = jax.nn.gelu(ya_ref[...])
    oa = g * jax.nn.sigmoid(_dot(g.astype(BF16), wglu_ref[...]) + bglu_ref[...])
    mx = functools.reduce(jnp.maximum, lses)
    es = [jnp.exp(l - mx) for l in lses]
    oc = sum(e * o for e, o in zip(es, ocs)) / sum(es)
    merged = (gates[0] * _dot(oa.astype(BF16), wba_ref[...])
              + gates[1] * _dot(ob.astype(BF16), wbb_ref[...])
              + gates[2] * _dot(oc.astype(BF16), wbc_ref[...]))
    o_ref[...] = x_ref[...] + _dot(merged.astype(BF16), wout_ref[...])


def merge(x, gate_col, ya, ob, ocs, lses, lw, *, tm=512):
    rows, d = x.shape
    tm = _row_tile(rows, tm)
    row = lambda w: pl.BlockSpec((tm, w), lambda i: (i, 0))
    full = lambda a: pl.BlockSpec(a.shape, lambda i: (0, 0))
    weights = (lw['g_mix'].astype(F32).reshape(1, -1), lw['w_in'][:, gate_col:gate_col + 3 * d].astype(BF16),
               lw['w_glu'].astype(BF16), lw['b_glu'].astype(F32).reshape(1, -1), lw['w_branch_a'].astype(BF16),
               lw['w_branch_b'].astype(BF16), lw['w_branch_c'].astype(BF16), lw['w_out'].astype(BF16))
    pairs = list(ob) + [a for grp in ocs for a in grp] + [a for grp in lses for a in grp]
    return pl.pallas_call(
        functools.partial(_merge_kernel, nb_pairs=len(ob), nc_pairs=len(ocs[0]), n_groups=len(ocs)),
        grid=(rows // tm,),
        in_specs=[row(d), row(ya.shape[1])] + [row(LANES)] * len(pairs) + [full(w) for w in weights],
        out_specs=row(d),
        out_shape=jax.ShapeDtypeStruct((rows, d), F32),
        compiler_params=_cparams("parallel"),
        name="merge",
    )(x, ya, *pairs, *weights)


def _mlp_ple_kernel(x_ref, g_ref, wup_ref, wdn_ref, p_ref, gp_ref, wg_ref, wp_ref, gf_ref, o_ref, *, final_norm, tk):
    x = x_ref[...]
    hn = _rmsnorm_bf16(x, g_ref[...])
    acc = None
    for k in range(wup_ref.shape[1] // tk):
        h = jnp.maximum(_dot(hn, wup_ref[:, k * tk:(k + 1) * tk]), 0.0)
        part = _dot((h * h).astype(BF16), wdn_ref[k * tk:(k + 1) * tk, :])
        acc = part if acc is None else acc + part
    x1 = x + acc
    gate = jax.nn.sigmoid(_dot(_rmsnorm_bf16(x1, gp_ref[...]), wg_ref[...]))
    y = x1 + gate * _dot(p_ref[...].astype(BF16), wp_ref[...])
    if final_norm:
        ms = jnp.mean(y * y, axis=-1, keepdims=True)
        y = y * lax.rsqrt(ms + EPS) * gf_ref[...]
    o_ref[...] = y


def mlp_ple(x, g_mlp, w_up, w_down, p, layer, g_ple, w_gate, w_proj, g_final, *, final_norm, tm=512, tk=1024):
    rows, d = x.shape
    dff = w_up.shape[1]
    tm = _row_tile(rows, tm)
    assert dff % tk == 0
    const = lambda a: pl.BlockSpec(a.shape, lambda i: (0, 0), pipeline_mode=pl.Buffered(1))
    return pl.pallas_call(
        functools.partial(_mlp_ple_kernel, final_norm=final_norm, tk=tk),
        grid=(rows // tm,),
        in_specs=[pl.BlockSpec((tm, d), lambda i: (i, 0)),
                  const(g_mlp), const(w_up), const(w_down),
                  pl.BlockSpec((None, tm, p.shape[2]), lambda i: (layer, i, 0)),
                  const(g_ple), const(w_gate), const(w_proj), const(g_final)],
        out_specs=pl.BlockSpec((tm, d), lambda i: (i, 0)),
        out_shape=jax.ShapeDtypeStruct((rows, d), F32),
        compiler_params=_cparams("parallel"),
        name="mlp_ple",
    )(x, g_mlp, w_up, w_down, p, g_ple, w_gate, w_proj, g_final)


def _column_offsets():
    ssm_w = 32 * SSM_GROUP
    swa_q, swa_kv, dil_w = SWA_Q_HEADS * HEAD_DIM, SWA_KV_HEADS * HEAD_DIM, DIL_HEADS * HEAD_DIM
    off = {'u': 0, 'swa_q': ssm_w, 'swa_k': ssm_w + swa_q, 'swa_v': ssm_w + swa_q + swa_kv}
    off['dil'] = ssm_w + swa_q + 2 * swa_kv
    off['gates'] = off['dil'] + 3 * len(DIL_PAIRS) * dil_w
    return off


def _prompt_mixer(z, fin, n, t, lw):
    off = _column_offsets()
    rows = n * t
    z3 = z.reshape(n, t, -1)
    dil_w = DIL_HEADS * HEAD_DIM

    groups = fin.shape[-1] // SSM_STATE
    ssm_new = jnp.stack([fin[:, 0].reshape(n, groups, SSM_STATE), fin[:, 1].reshape(n, groups, SSM_STATE)], axis=-1)

    flat = lambda pairs: [a.reshape(rows, LANES) for a in pairs]
    ob = flat(band_attn(z3, q_col=off['swa_q'], k_col=off['swa_k'], v_col=off['swa_v'],
                        q_heads=SWA_Q_HEADS, kv_heads=SWA_KV_HEADS, sinks=lw['sinks']))
    keep = min(BAND, t)
    swa_new = z3[:, t - keep:, off['swa_k']:off['swa_k'] + 2 * SWA_KV_HEADS * HEAD_DIM].reshape(
        n, keep, 2, SWA_KV_HEADS, HEAD_DIM)

    ocs, lses, dil_new = [], [], []
    for gi, (win, dil) in enumerate(DIL_PAIRS):
        assert win == BAND * dil
        c0 = off['dil'] + 3 * dil_w * gi
        o, lse = band_attn(z3, q_col=c0, k_col=c0 + dil_w, v_col=c0 + 2 * dil_w,
                           q_heads=DIL_HEADS, kv_heads=DIL_HEADS, dilation=dil, with_lse=True)
        ocs.append(flat(o))
        lses.append(flat(lse))
        keep = min(win, t)
        dil_new.append(z3[:, t - keep:, c0 + dil_w:c0 + 3 * dil_w].reshape(n, keep, 2, DIL_HEADS, HEAD_DIM))
    return ob, ocs, lses, (swa_new, dil_new[0], dil_new[1], dil_new[2], ssm_new)


def _sample_mixer(z, lw, cache):
    off = _column_offsets()
    n = z.shape[0]
    dil_w = DIL_HEADS * HEAD_DIM
    ssm_w = off['swa_q']
    layer = cache['layer']

    h0 = cache['ssm'].astype(F32)
    ya, s_r, s_i = ssm_step(z[:, :ssm_w], h0[..., 0].reshape(n, -1), h0[..., 1].reshape(n, -1), lw)
    ssm_new = jnp.stack([s_r.reshape(h0.shape[:-1]), s_i.reshape(h0.shape[:-1])], axis=-1)

    def attend(q_col, kv_col, q_heads, kv_heads, buf, dilation, sinks=None):
        grp = q_heads // kv_heads
        q = z[:, q_col:q_col + q_heads * HEAD_DIM].reshape(n, kv_heads, grp, HEAD_DIM)
        new_kv = z[:, kv_col:kv_col + 2 * kv_heads * HEAD_DIM].reshape(n, 2, kv_heads, HEAD_DIM)
        if sinks is not None:
            sinks = sinks.reshape(kv_heads, grp)
        o, lse = decode_attn(q, buf, layer, new_kv, dilation=dilation, sinks=sinks)
        pairs = lambda a: [a.reshape(n, q_heads * HEAD_DIM)[:, j * LANES:(j + 1) * LANES]
                           for j in range(q_heads * HEAD_DIM // LANES)]
        return pairs(o), pairs(lse), new_kv[:, None]

    ob, _, swa_new = attend(off['swa_q'], off['swa_k'], SWA_Q_HEADS, SWA_KV_HEADS, cache['swa'], 1,
                            sinks=lw['sinks'])
    ocs, lses, dil_new = [], [], []
    for gi, (win, dil) in enumerate(DIL_PAIRS):
        c0 = off['dil'] + 3 * dil_w * gi
        o, lse, new = attend(c0, c0 + dil_w, DIL_HEADS, DIL_HEADS, cache['dil'][gi], dil)
        ocs.append(o)
        lses.append(lse)
        dil_new.append(new)
    return ya, ob, ocs, lses, (swa_new, dil_new[0], dil_new[1], dil_new[2], ssm_new)


def _layer(x, p, layer, lw, cache, n, t, g_final, final_norm):
    off = _column_offsets()
    row1 = lambda a: a.astype(F32).reshape(1, -1)
    w_mix = lw['w_in'][:, :off['gates']].astype(BF16)
    if cache is None:
        z, ya, fin = proj_ssm(x, row1(lw['g_mix']), w_mix, n, t, lw)
        ob, ocs, lses, st = _prompt_mixer(z, fin, n, t, lw)
    else:
        assert t == 1
        win_rows = [c.shape[2] for c in (cache['swa'],) + tuple(cache['dil'])]
        assert win_rows == [BAND] + [w for w, _ in DIL_PAIRS], win_rows
        z = norm_matmul(x, row1(lw['g_mix']), w_mix)
        ya, ob, ocs, lses, st = _sample_mixer(z, lw, cache)
    x = merge(x, off['gates'], ya, ob, ocs, lses, lw)
    x = mlp_ple(x, row1(lw['g_mlp']), lw['w_up'].astype(BF16), lw['w_down'].astype(BF16), p, layer,
                row1(lw['g_ple']), lw['w_ple_gate'].astype(BF16), lw['w_ple_proj'].astype(BF16), row1(g_final),
                final_norm=final_norm)
    return x, st


def kernel(x_prompt, x_sample, cache_swa_kv, cache_dil_d1_kv, cache_dil_d4_kv, cache_dil_d16_kv, state_ssm, p_prompt, p_sample, w_in, g_mix, ssm_a_re, ssm_a_im, ssm_log_dt, ssm_b_re, ssm_b_im, ssm_c_re, ssm_c_im, ssm_d, w_glu, b_glu, attn_sinks, w_branch_a, w_branch_b, w_branch_c, w_out, g_mlp, w_up, w_down, g_ple, w_ple_gate, w_ple_proj, g_final):
    depth = w_in.shape[0]
    bp, tp, d = x_prompt.shape
    bs, ts, _ = x_sample.shape
    yp = x_prompt.reshape(bp * tp, d)
    ys = x_sample.reshape(bs * ts, d)
    st_p, st_s = [], []
    for l in range(depth):
        lw = {'w_in': w_in[l], 'g_mix': g_mix[l], 'a_re': ssm_a_re[l], 'a_im': ssm_a_im[l],
              'log_dt': ssm_log_dt[l], 'b_re': ssm_b_re[l], 'b_im': ssm_b_im[l], 'c_re': ssm_c_re[l],
              'c_im': ssm_c_im[l], 'd': ssm_d[l], 'w_glu': w_glu[l], 'b_glu': b_glu[l], 'sinks': attn_sinks[l],
              'w_branch_a': w_branch_a[l], 'w_branch_b': w_branch_b[l], 'w_branch_c': w_branch_c[l],
              'w_out': w_out[l], 'g_mlp': g_mlp[l], 'w_up': w_up[l], 'w_down': w_down[l], 'g_ple': g_ple[l],
              'w_ple_gate': w_ple_gate[l], 'w_ple_proj': w_ple_proj[l]}
        cache_l = {'layer': l, 'swa': cache_swa_kv,
                   'dil': (cache_dil_d1_kv, cache_dil_d4_kv, cache_dil_d16_kv),
                   'ssm': state_ssm[l]}
        last = l == depth - 1
        yp, sp = _layer(yp, p_prompt.reshape(depth, bp * tp, -1), l, lw, None, bp, tp, g_final, last)
        ys, ss = _layer(ys, p_sample.reshape(depth, bs * ts, -1), l, lw, cache_l, bs, ts, g_final, last)
        st_p.append(sp)
        st_s.append(ss)
    stk = lambda sts, i: jnp.stack([s[i] for s in sts])
    return (yp.reshape(bp, tp, d), ys.reshape(bs, ts, d),
            stk(st_p, 0), stk(st_p, 1), stk(st_p, 2), stk(st_p, 3), stk(st_p, 4),
            stk(st_s, 0), stk(st_s, 1), stk(st_s, 2), stk(st_s, 3), stk(st_s, 4))
```

```python
import functools

import jax
import jax.numpy as jnp
from jax import lax
from jax.experimental import pallas as pl
from jax.experimental.pallas import tpu as pltpu

HEAD_DIM = 64
SSM_GROUP = 16
SSM_STATE = 64
SWA_Q_HEADS = 8
SWA_KV_HEADS = 2
DIL_HEADS = 4
DIL_PAIRS = ((128, 1), (512, 4), (2048, 16))
BAND = 128
SSM_CHUNK = 16
SSM_TILE_GROUPS = 16
EPS = 1e-6
LANES = 128
NEG = -1e30
VMEM_LIMIT = 48 * 1024 * 1024
BF16 = jnp.bfloat16
F32 = jnp.float32


def _cparams(*sem):
    return pltpu.CompilerParams(dimension_semantics=sem, vmem_limit_bytes=VMEM_LIMIT)


def _rmsnorm_bf16(xf, g):
    ms = jnp.mean(xf * xf, axis=-1, keepdims=True)
    return (xf * lax.rsqrt(ms + EPS) * g).astype(BF16)


def _dot(a, b):
    return jnp.dot(a, b, preferred_element_type=F32)


def _row_tile(rows, pref):
    t = min(rows, pref)
    assert rows % t == 0, (rows, pref)
    return t


def _norm_matmul_kernel(x_ref, g_ref, w_ref, o_ref):
    o_ref[...] = _dot(_rmsnorm_bf16(x_ref[...], g_ref[...]), w_ref[...])


def norm_matmul(x, g, w, *, tm=512):
    rows, d = x.shape
    cols = w.shape[1]
    tm = _row_tile(rows, tm)
    return pl.pallas_call(
        _norm_matmul_kernel,
        grid=(rows // tm,),
        in_specs=[pl.BlockSpec((tm, d), lambda i: (i, 0)),
                  pl.BlockSpec((1, d), lambda i: (0, 0)),
                  pl.BlockSpec((d, cols), lambda i: (0, 0))],
        out_specs=pl.BlockSpec((tm, cols), lambda i: (i, 0)),
        out_shape=jax.ShapeDtypeStruct((rows, cols), F32),
        compiler_params=_cparams("parallel"),
        name="norm_matmul",
    )(x, g, w)


def _ssm_discretize(lw):
    a_re, a_im = lw['a_re'].astype(F32), lw['a_im'].astype(F32)
    dt = jnp.exp(lw['log_dt'].astype(F32))[:, None]
    mag = jnp.exp(a_re * dt)
    lam_r, lam_i = mag * jnp.cos(a_im * dt), mag * jnp.sin(a_im * dt)
    den = a_re * a_re + a_im * a_im
    zr = ((lam_r - 1.0) * a_re + lam_i * a_im) / den
    zi = (lam_i * a_re - (lam_r - 1.0) * a_im) / den
    b_re, b_im = lw['b_re'].astype(F32), lw['b_im'].astype(F32)
    bb_r = zr[..., None] * b_re - zi[..., None] * b_im
    bb_i = zr[..., None] * b_im + zi[..., None] * b_re
    return lam_r, lam_i, bb_r, bb_i


def _ssm_scan_operands(lw):
    lam_r, lam_i, bb_r, bb_i = _ssm_discretize(lw)
    g, p, h = bb_r.shape
    tg = SSM_TILE_GROUPS
    eye = jnp.eye(tg, dtype=F32)
    tile_in = lambda b: (b.reshape(g // tg, tg, p, h).transpose(0, 1, 3, 2)[:, :, :, None, :]
                         * eye[None, :, None, :, None]).reshape(g // tg, tg * h, tg * p)
    tile_out = lambda c: (c.reshape(g // tg, tg, h, p).transpose(0, 1, 3, 2)[:, :, :, None, :]
                          * eye[None, :, None, :, None]).reshape(g // tg, tg * p, tg * h)
    pw_r, pw_i = [lam_r], [lam_i]
    for _ in range(SSM_CHUNK - 1):
        pr, pi = pw_r[-1], pw_i[-1]
        pw_r.append(pr * lam_r - pi * lam_i)
        pw_i.append(pr * lam_i + pi * lam_r)
    pw = jnp.stack([jnp.stack(pw_r).reshape(SSM_CHUNK, g * p), jnp.stack(pw_i).reshape(SSM_CHUNK, g * p)])
    return (tile_in(bb_r).astype(BF16), tile_in(bb_i).astype(BF16),
            tile_out(lw['c_re'].astype(F32)).astype(BF16), tile_out(lw['c_im'].astype(F32)).astype(BF16),
            pw, lw['d'].astype(F32).reshape(1, g * h))


def _proj_ssm_kernel(x_ref, g_ref, w_ref, bbr_ref, bbi_ref, ccr_ref, cci_ref, pw_ref, d_ref, z_ref, y_ref, fin_ref,
                     uscr, xr, xi, cr, ci, carry, yperm, *, lane_block):
    tiles, tin, tst = bbr_ref.shape
    tb, sw = xr.shape
    n_ucols = uscr.shape[0]
    wu = n_ucols * LANES
    chunks = tb // SSM_CHUNK

    @pl.when(pl.program_id(1) == 0)
    def _():
        carry[...] = jnp.zeros_like(carry)

    hn = _rmsnorm_bf16(x_ref[...], g_ref[...])
    zu = _dot(hn, w_ref[:, :wu])
    z_ref[:, :wu] = zu
    for l in range(n_ucols):
        uscr[l] = zu[:, l * LANES:(l + 1) * LANES]
    z_ref[:, wu:] = _dot(hn, w_ref[:, wu:])

    u = jnp.concatenate(
        [jnp.concatenate([uscr.at[l][pl.ds(j, chunks, stride=SSM_CHUNK), :] for j in range(SSM_CHUNK)], axis=0)
         for l in range(n_ucols)], axis=-1)
    ub = u.astype(BF16)

    step_rows = lambda j: pl.ds(j * chunks, chunks)
    blocks_per_tile = tst // lane_block
    for lb in range(sw // lane_block):
        k, off = lb // blocks_per_tile, (lb % blocks_per_tile) * lane_block
        cols = pl.ds(lb * lane_block, lane_block)
        ubk = ub[:, k * tin:(k + 1) * tin]
        pr = _dot(ubk, bbr_ref[k, :, off:off + lane_block])
        pi = _dot(ubk, bbi_ref[k, :, off:off + lane_block])
        lr, li = pw_ref[0, 0:1, cols], pw_ref[1, 0:1, cols]
        sr, si = pr[0:chunks], pi[0:chunks]
        xr[step_rows(0), cols] = sr
        xi[step_rows(0), cols] = si
        for j in range(1, SSM_CHUNK):
            nr = lr * sr - li * si + pr[j * chunks:(j + 1) * chunks]
            ni = lr * si + li * sr + pi[j * chunks:(j + 1) * chunks]
            xr[step_rows(j), cols] = nr
            xi[step_rows(j), cols] = ni
            sr, si = nr, ni

    lcr, lci = pw_ref[0, SSM_CHUNK - 1:SSM_CHUNK, :], pw_ref[1, SSM_CHUNK - 1:SSM_CHUNK, :]

    def chunk_step(c, state):
        kr, ki = state
        cr[pl.ds(c, 1), :] = kr
        ci[pl.ds(c, 1), :] = ki
        last = (SSM_CHUNK - 1) * chunks + c
        return (lcr * kr - lci * ki + xr[pl.ds(last, 1), :], lcr * ki + lci * kr + xi[pl.ds(last, 1), :])

    kr, ki = lax.fori_loop(0, chunks, chunk_step, (carry[0:1, :], carry[1:2, :]))
    carry[0:1, :] = kr
    carry[1:2, :] = ki
    fin_ref[0:1, :] = kr
    fin_ref[1:2, :] = ki

    ys = [None] * tiles
    for lb in range(sw // lane_block):
        k, off = lb // blocks_per_tile, (lb % blocks_per_tile) * lane_block
        cols = pl.ds(lb * lane_block, lane_block)
        ckr, cki = cr[:, cols], ci[:, cols]
        sr, si = [], []
        for j in range(SSM_CHUNK):
            pr, pi = pw_ref[0, j:j + 1, cols], pw_ref[1, j:j + 1, cols]
            sr.append(xr[step_rows(j), cols] + (pr * ckr - pi * cki))
            si.append(xi[step_rows(j), cols] + (pr * cki + pi * ckr))
        part = (_dot(jnp.concatenate(sr, axis=0).astype(BF16), ccr_ref[k, off:off + lane_block, :])
                - _dot(jnp.concatenate(si, axis=0).astype(BF16), cci_ref[k, off:off + lane_block, :]))
        ys[k] = part if ys[k] is None else ys[k] + part

    per_tile = tin // LANES
    for k in range(tiles):
        y = ys[k] + u[:, k * tin:(k + 1) * tin] * d_ref[:, k * tin:(k + 1) * tin]
        for l in range(per_tile):
            yperm[k * per_tile + l] = y[:, l * LANES:(l + 1) * LANES]
    for c in range(chunks):
        for l in range(n_ucols):
            y_ref[c * SSM_CHUNK:(c + 1) * SSM_CHUNK, l * LANES:(l + 1) * LANES] = (
                yperm.at[l][pl.ds(c, SSM_CHUNK, stride=chunks), :])


def proj_ssm(x, g, w, n, t, lw, *, tb=512, lane_block=256):
    ops = _ssm_scan_operands(lw)
    bbr = ops[0]
    tiles, tin, tst = bbr.shape
    wu, sw = tiles * tin, tiles * tst
    d, cols = w.shape
    tb = _row_tile(t, tb)
    assert tb % (8 * SSM_CHUNK) == 0 and sw % lane_block == 0 and tin % LANES == 0
    nt = t // tb
    n_ucols = wu // LANES
    const = lambda a: pl.BlockSpec(a.shape, lambda i, k: (0,) * a.ndim, pipeline_mode=pl.Buffered(1))
    rows = lambda width: pl.BlockSpec((tb, width), lambda i, k: (i * nt + k, 0))
    return pl.pallas_call(
        functools.partial(_proj_ssm_kernel, lane_block=lane_block),
        grid=(n, nt),
        in_specs=[rows(d), const(g), const(w)] + [const(a) for a in ops],
        out_specs=[rows(cols), rows(wu), pl.BlockSpec((None, 2, sw), lambda i, k: (i, 0, 0))],
        out_shape=[jax.ShapeDtypeStruct((n * t, cols), F32), jax.ShapeDtypeStruct((n * t, wu), F32),
                   jax.ShapeDtypeStruct((n, 2, sw), F32)],
        scratch_shapes=[pltpu.VMEM((n_ucols, tb, LANES), F32),
                        pltpu.VMEM((tb, sw), F32), pltpu.VMEM((tb, sw), F32),
                        pltpu.VMEM((tb // SSM_CHUNK, sw), F32), pltpu.VMEM((tb // SSM_CHUNK, sw), F32),
                        pltpu.VMEM((2, sw), F32), pltpu.VMEM((n_ucols, tb, LANES), F32)],
        compiler_params=_cparams("parallel", "arbitrary"),
        name="proj_ssm",
    )(x, g, w, *ops)


def _ssm_step_kernel(u_ref, h0r_ref, h0i_ref, bbr_ref, bbi_ref, ccr_ref, cci_ref, lr_ref, li_ref, d_ref,
                     y_ref, sr_ref, si_ref):
    u = u_ref[...]
    ub = u.astype(BF16)
    lr, li = lr_ref[...], li_ref[...]
    h0r, h0i = h0r_ref[...], h0i_ref[...]
    sr = _dot(ub, bbr_ref[...]) + (lr * h0r - li * h0i)
    si = _dot(ub, bbi_ref[...]) + (lr * h0i + li * h0r)
    sr_ref[...] = sr
    si_ref[...] = si
    y_ref[...] = _dot(sr.astype(BF16), ccr_ref[...]) - _dot(si.astype(BF16), cci_ref[...]) + u * d_ref[...]


def ssm_step(u, h0r, h0i, lw):
    lam_r, lam_i, bb_r, bb_i = _ssm_discretize(lw)
    g, p, h = bb_r.shape
    eye = jnp.eye(g, dtype=F32)
    dense_in = lambda b: (b.transpose(0, 2, 1)[:, :, None, :] * eye[:, None, :, None]).reshape(g * h, g * p)
    dense_out = lambda c: (c.transpose(0, 2, 1)[:, :, None, :] * eye[:, None, :, None]).reshape(g * p, g * h)
    n = u.shape[0]
    args = (u, h0r, h0i, dense_in(bb_r).astype(BF16), dense_in(bb_i).astype(BF16),
            dense_out(lw['c_re'].astype(F32)).astype(BF16), dense_out(lw['c_im'].astype(F32)).astype(BF16),
            lam_r.reshape(1, g * p), lam_i.reshape(1, g * p), lw['d'].astype(F32).reshape(1, g * h))
    full = lambda a: pl.BlockSpec(a.shape, lambda i: (0, 0))
    return pl.pallas_call(
        _ssm_step_kernel,
        grid=(1,),
        in_specs=[full(a) for a in args],
        out_specs=[pl.BlockSpec((n, g * h), lambda i: (0, 0)),
                   pl.BlockSpec((n, g * p), lambda i: (0, 0)),
                   pl.BlockSpec((n, g * p), lambda i: (0, 0))],
        out_shape=[jax.ShapeDtypeStruct((n, g * h), F32),
                   jax.ShapeDtypeStruct((n, g * p), F32),
                   jax.ShapeDtypeStruct((n, g * p), F32)],
        compiler_params=_cparams("arbitrary"),
        name="ssm_step",
    )(*args)


def _band_attn_kernel(*refs, n_pairs, kv_pairs, heads, dilation, units, with_sink, with_lse):
    it = iter(refs)
    q_refs = [next(it) for _ in range(n_pairs)]
    kv_refs = [[next(it) for _ in range(4)] for _ in range(kv_pairs)]
    sink_ref = next(it) if with_sink else None
    o_refs = [next(it) for _ in range(n_pairs)]
    lse_refs = [next(it) for _ in range(n_pairs)] if with_lse else None
    b = pl.program_id(1)
    kj = lax.broadcasted_iota(jnp.int32, (2 * BAND, BAND), 0)
    qi = lax.broadcasted_iota(jnp.int32, (2 * BAND, BAND), 1)
    in_band = (kj >= qi) & (kj <= qi + BAND)
    bias_rest = jnp.where(in_band, 0.0, NEG)
    bias_first = jnp.where(in_band & ((kj >= BAND) | (b > 0)), 0.0, NEG)
    lane = lax.broadcasted_iota(jnp.int32, (1, LANES), 1)
    lane_half = [(lane < HEAD_DIM).astype(F32), (lane >= HEAD_DIM).astype(F32)]
    row = lax.broadcasted_iota(jnp.int32, (LANES, 1), 0)
    row_half = [row < HEAD_DIM, row >= HEAD_DIM]

    members = {}
    for j in range(n_pairs):
        for half in (0, 1):
            members.setdefault(heads[j][half], []).append((half, j))

    unit = BAND * dilation
    strided = lambda start: pl.ds(start, BAND, stride=dilation) if dilation > 1 else pl.ds(start, BAND)

    def block(r, i):
        rows = strided(i * unit + r)
        bias = bias_first if i == 0 else bias_rest

        def with_prev(prev_ref, cur_ref):
            prev = cur_ref[strided((i - 1) * unit + r), :] if i > 0 else prev_ref[strided(r), :]
            return jnp.concatenate([prev, cur_ref[rows, :]], axis=0)

        q2 = [q_refs[j][rows, :] * (HEAD_DIM ** -0.5) for j in range(n_pairs)]
        ot = [jnp.zeros((LANES, BAND), F32) for _ in range(n_pairs)]
        lset = [jnp.zeros((LANES, BAND), F32) for _ in range(n_pairs)]
        base = {}
        for (pair, rot), group in members.items():
            group = sorted(group)
            if pair not in base:
                kp_ref, kc_ref, vp_ref, vc_ref = kv_refs[pair]
                base[pair] = (with_prev(kp_ref, kc_ref), with_prev(vp_ref, vc_ref).T)
            k2, vt = base[pair]
            if rot:
                k2 = pltpu.roll(k2, HEAD_DIM, 1)
                vt = jnp.concatenate([vt[HEAD_DIM:], vt[:HEAD_DIM]], axis=0)
            nh = len(group)
            qm = jnp.concatenate([(q2[j] * lane_half[half]).astype(BF16) for half, j in group], axis=0)
            st = lax.dot_general(k2.astype(BF16), qm, (((1,), (1,)), ((), ())), preferred_element_type=F32)
            st = st + jnp.concatenate([bias] * nh, axis=1)
            m = jnp.max(st, axis=0, keepdims=True)
            e = jnp.exp(st - m)
            lse = m + jnp.log(jnp.sum(e, axis=0, keepdims=True))
            if with_sink:
                sk = jnp.concatenate([jnp.full((1, BAND), sink_ref[2 * j + half], F32) for half, j in group], axis=1)
                mx = jnp.maximum(lse, sk)
                lse = mx + jnp.log(jnp.exp(lse - mx) + jnp.exp(sk - mx))
            pt = (e * jnp.exp(m - lse)).astype(BF16)
            for half in (0, 1):
                idx = [g for g, (hf, _) in enumerate(group) if hf == half]
                if not idx:
                    continue
                vb = jnp.where(row_half[half], vt, 0.0).astype(BF16)
                res = _dot(vb, pt[:, idx[0] * BAND:(idx[-1] + 1) * BAND])
                for n_, g in enumerate(idx):
                    j = group[g][1]
                    ot[j] = ot[j] + res[:, n_ * BAND:(n_ + 1) * BAND]
                    if with_lse:
                        lset[j] = jnp.where(row_half[half], lse[:, g * BAND:(g + 1) * BAND], lset[j])
        for j in range(n_pairs):
            o_refs[j][rows, :] = ot[j].T
            if with_lse:
                lse_refs[j][rows, :] = lset[j].T

    for i in range(units):
        for r in range(dilation):
            block(r, i)


def band_attn(src, *, q_col, k_col, v_col, q_heads, kv_heads, dilation=1, sinks=None, with_lse=False, nblk=16):
    s_, t_, _ = src.shape
    unit = BAND * dilation
    units = max(1, nblk // dilation)
    blk = unit * units
    assert t_ % blk == 0 and q_heads % 2 == 0 and kv_heads % 2 == 0
    grp = q_heads // kv_heads
    n_pairs = q_heads // 2
    kv_pairs = kv_heads // 2
    assert q_col % LANES == 0 and k_col % LANES == 0 and v_col % LANES == 0
    heads = []
    for j in range(n_pairs):
        pair_heads = []
        for half in (0, 1):
            kvh = (2 * j + half) // grp
            pair_heads.append((kvh // 2, (kvh % 2) != half))
        heads.append(tuple(pair_heads))
    cur = lambda col: pl.BlockSpec((None, blk, LANES), lambda n, b: (n, b, col // LANES))
    prev = lambda col: pl.BlockSpec((None, unit, LANES),
                                    lambda n, b: (n, jnp.maximum(b * units - 1, 0), col // LANES))
    in_specs = [cur(q_col + j * LANES) for j in range(n_pairs)]
    for kp in range(kv_pairs):
        kc, vc = k_col + kp * LANES, v_col + kp * LANES
        in_specs += [prev(kc), cur(kc), prev(vc), cur(vc)]
    args = [src] * len(in_specs)
    if sinks is not None:
        in_specs.append(pl.BlockSpec(memory_space=pltpu.SMEM))
        args.append(sinks.astype(F32))
    n_out = n_pairs * (2 if with_lse else 1)
    out = pl.pallas_call(
        functools.partial(_band_attn_kernel, n_pairs=n_pairs, kv_pairs=kv_pairs, heads=tuple(heads),
                          dilation=dilation, units=units, with_sink=sinks is not None, with_lse=with_lse),
        grid=(s_, t_ // blk),
        in_specs=in_specs,
        out_specs=[pl.BlockSpec((None, blk, LANES), lambda n, b: (n, b, 0))] * n_out,
        out_shape=[jax.ShapeDtypeStruct((s_, t_, LANES), F32)] * n_out,
        compiler_params=_cparams("parallel", "arbitrary"),
        name="band_attn",
    )(*args)
    return (out[:n_pairs], out[n_pairs:]) if with_lse else out


def _decode_attn_kernel(*refs, nb, hkv, dilation, with_sink):
    it = iter(refs)
    q_ref, kv_ref, new_ref = next(it), next(it), next(it)
    sink_ref = next(it) if with_sink else None
    o_ref, lse_ref = next(it), next(it)
    rows = kv_ref.shape[-1]
    gq = q_ref.shape[2]
    pos = lax.broadcasted_iota(jnp.int32, (1, rows), 1)
    bias = jnp.where((pos & (dilation - 1)) == 0, 0.0, NEG)
    scale = HEAD_DIM ** -0.5

    sb = max(d for d in (1, 2, 4) if nb % d == 0)
    heads = [(j, h) for j in range(sb) for h in range(hkv)]

    def per_group(it, c):
        i0 = it * sb
        qs = [q_ref[i0 + j, h] * scale for j, h in heads]
        new = [new_ref[i0 + j] for j in range(sb)]
        s = jnp.concatenate([_dot(q.astype(BF16), kv_ref[i0 + j, 0, h].astype(BF16))
                             for q, (j, h) in zip(qs, heads)], axis=0) + bias
        sn = jnp.concatenate([jnp.sum(q * new[j][0, h:h + 1, :], axis=-1, keepdims=True)
                              for q, (j, h) in zip(qs, heads)], axis=0)
        m = jnp.maximum(jnp.max(s, axis=-1, keepdims=True), sn)
        e, en = jnp.exp(s - m), jnp.exp(sn - m)
        lse = m + jnp.log(jnp.sum(e, axis=-1, keepdims=True) + en)
        if with_sink:
            sk = jnp.concatenate([sink_ref[...]] * sb, axis=0)
            mx = jnp.maximum(lse, sk)
            lse = mx + jnp.log(jnp.exp(lse - mx) + jnp.exp(sk - mx))
        w = jnp.exp(m - lse)
        p, pn = (e * w).astype(BF16), en * w
        for n_, (j, h) in enumerate(heads):
            hs = slice(n_ * gq, (n_ + 1) * gq)
            o = lax.dot_general(p[hs], kv_ref[i0 + j, 1, h].astype(BF16), (((1,), (1,)), ((), ())),
                                preferred_element_type=F32)
            o_ref[i0 + j, h] = o + pn[hs] * new[j][1, h:h + 1, :]
            lse_ref[i0 + j, h] = jnp.broadcast_to(lse[hs], (gq, HEAD_DIM))
        return c

    lax.fori_loop(0, nb // sb, per_group, 0)


def decode_attn(q, cache, layer, new_kv, *, dilation, sinks=None, block_bytes=8 * 1024 * 1024):
    n, hkv, grp, dh = q.shape
    rows = cache.shape[2]
    gq = -(-grp // 8) * 8
    assert dilation & (dilation - 1) == 0
    nb = max(1, min(n, block_bytes // (2 * hkv * dh * rows * 4)))
    while n % nb:
        nb -= 1
    pad_g = lambda a: jnp.pad(a, [(0, 0)] * (a.ndim - 2) + [(0, gq - grp), (0, 0)])
    cache_t = cache.transpose(0, 1, 3, 4, 5, 2)
    in_specs = [pl.BlockSpec((nb, hkv, gq, dh), lambda i: (i, 0, 0, 0)),
                pl.BlockSpec((None, nb, 2, hkv, dh, rows), lambda i: (layer, i, 0, 0, 0, 0)),
                pl.BlockSpec((nb, 2, hkv, dh), lambda i: (i, 0, 0, 0))]
    args = [pad_g(q), cache_t, new_kv]
    if sinks is not None:
        in_specs.append(pl.BlockSpec((hkv * gq, 1), lambda i: (0, 0)))
        args.append(pad_g(sinks.astype(F32)[:, :, None]).reshape(hkv * gq, 1))
    out_spec = pl.BlockSpec((nb, hkv, gq, dh), lambda i: (i, 0, 0, 0))
    out_sds = jax.ShapeDtypeStruct((n, hkv, gq, dh), F32)
    o, lse = pl.pallas_call(
        functools.partial(_decode_attn_kernel, nb=nb, hkv=hkv, dilation=dilation, with_sink=sinks is not None),
        grid=(n // nb,),
        in_specs=in_specs,
        out_specs=[out_spec, out_spec],
        out_shape=[out_sds, out_sds],
        compiler_params=_cparams("parallel"),
        name="decode_attn",
    )(*args)
    return o[:, :, :grp], lse[:, :, :grp]


def _merge_kernel(*refs, nb_pairs, nc_pairs, n_groups):
    it = iter(refs)
    x_ref, ya_ref = next(it), next(it)
    lanes = lambda k: jnp.concatenate([next(it)[...] for _ in range(k)], axis=-1)
    ob = lanes(nb_pairs)
    ocs = [lanes(nc_pairs) for _ in range(n_groups)]
    lses = [lanes(nc_pairs) for _ in range(n_groups)]
    gmix_ref, wgate_ref = next(it), next(it)
    wglu_ref, bglu_ref, wba_ref, wbb_ref, wbc_ref, wout_ref, o_ref = (next(it) for _ in range(7))
    d = x_ref.shape[1]
    hn = _rmsnorm_bf16(x_ref[...], gmix_ref[...])
    gates = [jax.nn.sigmoid(_dot(hn, wgate_ref[:, b * d:(b + 1) * d])) for b in range(3)]
    g = jax.nn.gelu(ya_ref[...])
    oa = g * jax.nn.sigmoid(_dot(g.astype(BF16), wglu_ref[...]) + bglu_ref[...])
    mx = functools.reduce(jnp.maximum, lses)
    es = [jnp.exp(l - mx) for l in lses]
    oc = sum(e * o for e, o in zip(es, ocs)) / sum(es)
    merged = (gates[0] * _dot(oa.astype(BF16), wba_ref[...])
              + gates[1] * _dot(ob.astype(BF16), wbb_ref[...])
              + gates[2] * _dot(oc.astype(BF16), wbc_ref[...]))
    o_ref[...] = x_ref[...] + _dot(merged.astype(BF16), wout_ref[...])


def merge(x, gate_col, ya, ob, ocs, lses, lw, *, tm=512):
    rows, d = x.shape
    tm = _row_tile(rows, tm)
    row = lambda w: pl.BlockSpec((tm, w), lambda i: (i, 0))
    full = lambda a: pl.BlockSpec(a.shape, lambda i: (0, 0))
    weights = (lw['g_mix'].astype(F32).reshape(1, -1), lw['w_in'][:, gate_col:gate_col + 3 * d].astype(BF16),
               lw['w_glu'].astype(BF16), lw['b_glu'].astype(F32).reshape(1, -1), lw['w_branch_a'].astype(BF16),
               lw['w_branch_b'].astype(BF16), lw['w_branch_c'].astype(BF16), lw['w_out'].astype(BF16))
    pairs = list(ob) + [a for grp in ocs for a in grp] + [a for grp in lses for a in grp]
    return pl.pallas_call(
        functools.partial(_merge_kernel, nb_pairs=len(ob), nc_pairs=len(ocs[0]), n_groups=len(ocs)),
        grid=(rows // tm,),
        in_specs=[row(d), row(ya.shape[1])] + [row(LANES)] * len(pairs) + [full(w) for w in weights],
        out_specs=row(d),
        out_shape=jax.ShapeDtypeStruct((rows, d), F32),
        compiler_params=_cparams("parallel"),
        name="merge",
    )(x, ya, *pairs, *weights)


def _mlp_ple_kernel(x_ref, g_ref, wup_ref, wdn_ref, p_ref, gp_ref, wg_ref, wp_ref, gf_ref, o_ref, *, final_norm, tk):
    x = x_ref[...]
    hn = _rmsnorm_bf16(x, g_ref[...])
    acc = None
    for k in range(wup_ref.shape[1] // tk):
        h = jnp.maximum(_dot(hn, wup_ref[:, k * tk:(k + 1) * tk]), 0.0)
        part = _dot((h * h).astype(BF16), wdn_ref[k * tk:(k + 1) * tk, :])
        acc = part if acc is None else acc + part
    x1 = x + acc
    gate = jax.nn.sigmoid(_dot(_rmsnorm_bf16(x1, gp_ref[...]), wg_ref[...]))
    y = x1 + gate * _dot(p_ref[...].astype(BF16), wp_ref[...])
    if final_norm:
        ms = jnp.mean(y * y, axis=-1, keepdims=True)
        y = y * lax.rsqrt(ms + EPS) * gf_ref[...]
    o_ref[...] = y


def mlp_ple(x, g_mlp, w_up, w_down, p, layer, g_ple, w_gate, w_proj, g_final, *, final_norm, tm=512, tk=1024):
    rows, d = x.shape
    dff = w_up.shape[1]
    tm = _row_tile(rows, tm)
    assert dff % tk == 0
    const = lambda a: pl.BlockSpec(a.shape, lambda i: (0, 0), pipeline_mode=pl.Buffered(1))
    return pl.pallas_call(
        functools.partial(_mlp_ple_kernel, final_norm=final_norm, tk=tk),
        grid=(rows // tm,),
        in_specs=[pl.BlockSpec((tm, d), lambda i: (i, 0)),
                  const(g_mlp), const(w_up), const(w_down),
                  pl.BlockSpec((None, tm, p.shape[2]), lambda i: (layer, i, 0)),
                  const(g_ple), const(w_gate), const(w_proj), const(g_final)],
        out_specs=pl.BlockSpec((tm, d), lambda i: (i, 0)),
        out_shape=jax.ShapeDtypeStruct((rows, d), F32),
        compiler_params=_cparams("parallel"),
        name="mlp_ple",
    )(x, g_mlp, w_up, w_down, p, g_ple, w_gate, w_proj, g_final)


def _column_offsets():
    ssm_w = 32 * SSM_GROUP
    swa_q, swa_kv, dil_w = SWA_Q_HEADS * HEAD_DIM, SWA_KV_HEADS * HEAD_DIM, DIL_HEADS * HEAD_DIM
    off = {'u': 0, 'swa_q': ssm_w, 'swa_k': ssm_w + swa_q, 'swa_v': ssm_w + swa_q + swa_kv}
    off['dil'] = ssm_w + swa_q + 2 * swa_kv
    off['gates'] = off['dil'] + 3 * len(DIL_PAIRS) * dil_w
    return off


def _prompt_mixer(z, fin, n, t, lw):
    off = _column_offsets()
    rows = n * t
    z3 = z.reshape(n, t, -1)
    dil_w = DIL_HEADS * HEAD_DIM

    groups = fin.shape[-1] // SSM_STATE
    ssm_new = jnp.stack([fin[:, 0].reshape(n, groups, SSM_STATE), fin[:, 1].reshape(n, groups, SSM_STATE)], axis=-1)

    flat = lambda pairs: [a.reshape(rows, LANES) for a in pairs]
    ob = flat(band_attn(z3, q_col=off['swa_q'], k_col=off['swa_k'], v_col=off['swa_v'],
                        q_heads=SWA_Q_HEADS, kv_heads=SWA_KV_HEADS, sinks=lw['sinks']))
    keep = min(BAND, t)
    swa_new = z3[:, t - keep:, off['swa_k']:off['swa_k'] + 2 * SWA_KV_HEADS * HEAD_DIM].reshape(
        n, keep, 2, SWA_KV_HEADS, HEAD_DIM)

    ocs, lses, dil_new = [], [], []
    for gi, (win, dil) in enumerate(DIL_PAIRS):
        assert win == BAND * dil
        c0 = off['dil'] + 3 * dil_w * gi
        o, lse = band_attn(z3, q_col=c0, k_col=c0 + dil_w, v_col=c0 + 2 * dil_w,
                           q_heads=DIL_HEADS, kv_heads=DIL_HEADS, dilation=dil, with_lse=True)
        ocs.append(flat(o))
        lses.append(flat(lse))
        keep = min(win, t)
        dil_new.append(z3[:, t - keep:, c0 + dil_w:c0 + 3 * dil_w].reshape(n, keep, 2, DIL_HEADS, HEAD_DIM))
    return ob, ocs, lses, (swa_new, dil_new[0], dil_new[1], dil_new[2], ssm_new)


def _sample_mixer(z, lw, cache):
    off = _column_offsets()
    n = z.shape[0]
    dil_w = DIL_HEADS * HEAD_DIM
    ssm_w = off['swa_q']
    layer = cache['layer']

    h0 = cache['ssm'].astype(F32)
    ya, s_r, s_i = ssm_step(z[:, :ssm_w], h0[..., 0].reshape(n, -1), h0[..., 1].reshape(n, -1), lw)
    ssm_new = jnp.stack([s_r.reshape(h0.shape[:-1]), s_i.reshape(h0.shape[:-1])], axis=-1)

    def attend(q_col, kv_col, q_heads, kv_heads, buf, dilation, sinks=None):
        grp = q_heads // kv_heads
        q = z[:, q_col:q_col + q_heads * HEAD_DIM].reshape(n, kv_heads, grp, HEAD_DIM)
        new_kv = z[:, kv_col:kv_col + 2 * kv_heads * HEAD_DIM].reshape(n, 2, kv_heads, HEAD_DIM)
        if sinks is not None:
            sinks = sinks.reshape(kv_heads, grp)
        o, lse = decode_attn(q, buf, layer, new_kv, dilation=dilation, sinks=sinks)
        pairs = lambda a: [a.reshape(n, q_heads * HEAD_DIM)[:, j * LANES:(j + 1) * LANES]
                           for j in range(q_heads * HEAD_DIM // LANES)]
        return pairs(o), pairs(lse), new_kv[:, None]

    ob, _, swa_new = attend(off['swa_q'], off['swa_k'], SWA_Q_HEADS, SWA_KV_HEADS, cache['swa'], 1,
                            sinks=lw['sinks'])
    ocs, lses, dil_new = [], [], []
    for gi, (win, dil) in enumerate(DIL_PAIRS):
        c0 = off['dil'] + 3 * dil_w * gi
        o, lse, new = attend(c0, c0 + dil_w, DIL_HEADS, DIL_HEADS, cache['dil'][gi], dil)
        ocs.append(o)
        lses.append(lse)
        dil_new.append(new)
    return ya, ob, ocs, lses, (swa_new, dil_new[0], dil_new[1], dil_new[2], ssm_new)


def _layer(x, p, layer, lw, cache, n, t, g_final, final_norm):
    off = _column_offsets()
    row1 = lambda a: a.astype(F32).reshape(1, -1)
    w_mix = lw['w_in'][:, :off['gates']].astype(BF16)
    if cache is None:
        z, ya, fin = proj_ssm(x, row1(lw['g_mix']), w_mix, n, t, lw)
        ob, ocs, lses, st = _prompt_mixer(z, fin, n, t, lw)
    else:
        assert t == 1
        win_rows = [c.shape[2] for c in (cache['swa'],) + tuple(cache['dil'])]
        assert win_rows == [BAND] + [w for w, _ in DIL_PAIRS], win_rows
        z = norm_matmul(x, row1(lw['g_mix']), w_mix)
        ya, ob, ocs, lses, st = _sample_mixer(z, lw, cache)
    x = merge(x, off['gates'], ya, ob, ocs, lses, lw)
    x = mlp_ple(x, row1(lw['g_mlp']), lw['w_up'].astype(BF16), lw['w_down'].astype(BF16), p, layer,
                row1(lw['g_ple']), lw['w_ple_gate'].astype(BF16), lw['w_ple_proj'].astype(BF16), row1(g_final),
                final_norm=final_norm)
    return x, st


def kernel(x_prompt, x_sample, cache_swa_kv, cache_dil_d1_kv, cache_dil_d4_kv, cache_dil_d16_kv, state_ssm, p_prompt, p_sample, w_in, g_mix, ssm_a_re, ssm_a_im, ssm_log_dt, ssm_b_re, ssm_b_im, ssm_c_re, ssm_c_im, ssm_d, w_glu, b_glu, attn_sinks, w_branch_a, w_branch_b, w_branch_c, w_out, g_mlp, w_up, w_down, g_ple, w_ple_gate, w_ple_proj, g_final):
    depth = w_in.shape[0]
    bp, tp, d = x_prompt.shape
    bs, ts, _ = x_sample.shape
    yp = x_prompt.reshape(bp * tp, d)
    ys = x_sample.reshape(bs * ts, d)
    st_p, st_s = [], []
    for l in range(depth):
        lw = {'w_in': w_in[l], 'g_mix': g_mix[l], 'a_re': ssm_a_re[l], 'a_im': ssm_a_im[l],
              'log_dt': ssm_log_dt[l], 'b_re': ssm_b_re[l], 'b_im': ssm_b_im[l], 'c_re': ssm_c_re[l],
              'c_im': ssm_c_im[l], 'd': ssm_d[l], 'w_glu': w_glu[l], 'b_glu': b_glu[l], 'sinks': attn_sinks[l],
              'w_branch_a': w_branch_a[l], 'w_branch_b': w_branch_b[l], 'w_branch_c': w_branch_c[l],
              'w_out': w_out[l], 'g_mlp': g_mlp[l], 'w_up': w_up[l], 'w_down': w_down[l], 'g_ple': g_ple[l],
              'w_ple_gate': w_ple_gate[l], 'w_ple_proj': w_ple_proj[l]}
        cache_l = {'layer': l, 'swa': cache_swa_kv,
                   'dil': (cache_dil_d1_kv, cache_dil_d4_kv, cache_dil_d16_kv),
                   'ssm': state_ssm[l]}
        last = l == depth - 1
        yp, sp = _layer(yp, p_prompt.reshape(depth, bp * tp, -1), l, lw, None, bp, tp, g_final, last)
        ys, ss = _layer(ys, p_sample.reshape(depth, bs * ts, -1), l, lw, cache_l, bs, ts, g_final, last)
        st_p.append(sp)
        st_s.append(ss)
    stk = lambda sts, i: jnp.stack([s[i] for s in sts])
    return (yp.reshape(bp, tp, d), ys.reshape(bs, ts, d),
            stk(st_p, 0), stk(st_p, 1), stk(st_p, 2), stk(st_p, 3), stk(st_p, 4),
            stk(st_s, 0), stk(st_s, 1), stk(st_s, 2), stk(st_s, 3), stk(st_s, 4))
```
